```python
import math
import jax, jax.numpy as jnp
from jax import lax
import numpy as np

D_MODEL = 2048
BATCH = 2
SEQ = 16384
DEPTH = 2

CHUNK = 64
N_MIXERS = 2
N_GLA_LAYERS = (DEPTH + N_MIXERS - 1) // N_MIXERS
N_DIFF_LAYERS = DEPTH // N_MIXERS
RMS_EPS = 1e-6

GLA_HEADS = 4
GLA_DK_TOTAL = D_MODEL // 2
GLA_DV_TOTAL = D_MODEL
GLA_DK = GLA_DK_TOTAL // GLA_HEADS
GLA_DV = GLA_DV_TOTAL // GLA_HEADS
GLA_GATE_RANK = 16
GLA_GATE_TAU = 16.0
GLA_IN_WIDTH = 2 * GLA_DK_TOTAL + 2 * GLA_DV_TOTAL + GLA_GATE_RANK

DIFF_HEADS = 8
DIFF_HEAD_DIM = D_MODEL // DIFF_HEADS // 2
DIFF_V_DIM = 2 * DIFF_HEAD_DIM
Q_BLOCK = 128

MOE_GROUPS = 4
MOE_EXPERTS_PER_GROUP = 8
MOE_N_EXPERTS = MOE_GROUPS * MOE_EXPERTS_PER_GROUP
MOE_TOP_K = 2
MOE_D_FF = D_MODEL // 4
MOE_BLOCK = 128

kernel_name = "hybrid_gla_diffattn_hmoe_stream"


def _rmsnorm(x, g):
    xf = x.astype(jnp.float32)
    y = xf * lax.rsqrt(jnp.mean(xf * xf, axis=-1, keepdims=True) + RMS_EPS)
    return y.astype(x.dtype) * g.astype(x.dtype)


def _gla_mixer(h, w_in, w_gate_up, b_gate, norm_g, w_out):
    b, s, _ = h.shape
    n_chunks = s // CHUNK
    proj = h @ w_in
    q, k, v, r, gz = jnp.split(
        proj,
        [GLA_DK_TOTAL, 2 * GLA_DK_TOTAL, 2 * GLA_DK_TOTAL + GLA_DV_TOTAL,
         2 * GLA_DK_TOTAL + 2 * GLA_DV_TOTAL],
        axis=-1)
    log_alpha = jax.nn.log_sigmoid((gz @ w_gate_up + b_gate).astype(jnp.float32)) / GLA_GATE_TAU

    def to_chunks(t, d):
        return t.astype(jnp.float32).reshape(b, n_chunks, CHUNK, GLA_HEADS, d).transpose(1, 0, 3, 2, 4)

    qc = to_chunks(q, GLA_DK) * (GLA_DK ** -0.5)
    kc = to_chunks(k, GLA_DK)
    vc = to_chunks(v, GLA_DV)
    ac = to_chunks(log_alpha, GLA_DK)

    def step(state, xs):
        q_c, k_c, v_c, la_c = xs
        cum = jnp.cumsum(la_c, axis=2)
        cum_end = cum[:, :, -1:, :]
        k_dec = k_c * jnp.exp(cum_end - cum)
        state = (jnp.exp(cum_end[:, :, 0, :])[..., None] * state
                 + jnp.einsum('bhck,bhcv->bhkv', k_dec, v_c))
        o_c = jnp.einsum('bhck,bhkv->bhcv', q_c, state)
        return state, o_c

    state0 = jnp.zeros((b, GLA_HEADS, GLA_DK, GLA_DV), jnp.float32)
    _, o = lax.scan(step, state0, (qc, kc, vc, ac))
    o = o.transpose(1, 0, 3, 2, 4).reshape(b, s, GLA_HEADS, GLA_DV).astype(h.dtype)
    o = _rmsnorm(o, norm_g).reshape(b, s, GLA_DV_TOTAL) * jax.nn.silu(r)
    return o @ w_out


def _diff_attention(h, w_in, lq1, lk1, lq2, lk2, subln, w_out, layer_idx):
    b, s, _ = h.shape
    lambda_init = 0.8 - 0.6 * math.exp(-0.3 * layer_idx)
    qkv = h @ w_in
    q, k, v = jnp.split(qkv, 3, axis=-1)
    q = q.reshape(b, s, DIFF_HEADS, 2, DIFF_HEAD_DIM).transpose(3, 0, 2, 1, 4)
    k = k.reshape(b, s, DIFF_HEADS, 2, DIFF_HEAD_DIM).transpose(3, 0, 2, 1, 4)
    v = v.reshape(b, s, DIFF_HEADS, DIFF_V_DIM).transpose(0, 2, 1, 3)
    q1, q2 = q[0], q[1]
    k1, k2 = k[0], k[1]
    lam = (jnp.exp(jnp.sum(lq1.astype(jnp.float32) * lk1.astype(jnp.float32)))
           - jnp.exp(jnp.sum(lq2.astype(jnp.float32) * lk2.astype(jnp.float32)))
           + lambda_init)
    slopes = 2.0 ** (-8.0 * jnp.arange(1, DIFF_HEADS + 1, dtype=jnp.float32) / DIFF_HEADS)
    scale = DIFF_HEAD_DIM ** -0.5
    key_pos = jnp.arange(s, dtype=jnp.int32)
    key_chunk = key_pos // CHUNK

    def q_block(qi):
        start = qi * Q_BLOCK
        q1b = lax.dynamic_slice_in_dim(q1, start, Q_BLOCK, axis=2)
        q2b = lax.dynamic_slice_in_dim(q2, start, Q_BLOCK, axis=2)
        q_pos = start + jnp.arange(Q_BLOCK, dtype=jnp.int32)
        dist = jnp.abs(q_pos[:, None] - key_pos[None, :]).astype(jnp.float32)
        allowed = (q_pos // CHUNK)[:, None] >= key_chunk[None, :]
        bias = jnp.where(allowed[None], -slopes[:, None, None] * dist[None], -jnp.inf)
        s1 = jnp.einsum('bhqd,bhkd->bhqk', q1b, k1).astype(jnp.float32) * scale + bias
        s2 = jnp.einsum('bhqd,bhkd->bhqk', q2b, k2).astype(jnp.float32) * scale + bias
        attn = jax.nn.softmax(s1, axis=-1) - lam * jax.nn.softmax(s2, axis=-1)
        return jnp.einsum('bhqk,bhkv->bhqv', attn.astype(v.dtype), v)

    o = lax.map(q_block, jnp.arange(s // Q_BLOCK, dtype=jnp.int32))
    o = o.transpose(1, 0, 3, 2, 4).reshape(b, s, DIFF_HEADS, DIFF_V_DIM).astype(h.dtype)
    o = _rmsnorm(o, subln) * (1.0 - lambda_init)
    return o.reshape(b, s, D_MODEL) @ w_out


def _hier_moe(h, w_group, w_expert, w_gate, w_up, w_down):
    b, s, d = h.shape
    n_tok = b * s
    xf = h.reshape(n_tok, d)
    tok_ids = jnp.arange(n_tok, dtype=jnp.int32)
    group_logits = (xf @ w_group).astype(jnp.float32)
    group_prob = jax.nn.softmax(group_logits, axis=-1)
    g_idx = jnp.argmax(group_logits, axis=-1).astype(jnp.int32)
    g_w = group_prob[tok_ids, g_idx]
    exp_logits = (xf @ w_expert).astype(jnp.float32).reshape(n_tok, MOE_GROUPS, MOE_EXPERTS_PER_GROUP)
    sel_logits = exp_logits[tok_ids, g_idx]
    top_p, top_e = lax.top_k(jax.nn.softmax(sel_logits, axis=-1), MOE_TOP_K)
    top_p = top_p / jnp.sum(top_p, axis=-1, keepdims=True)
    weights = g_w[:, None] * top_p

    eid = (g_idx[:, None] * MOE_EXPERTS_PER_GROUP + top_e.astype(jnp.int32)).reshape(-1)
    tok = jnp.broadcast_to(tok_ids[:, None], (n_tok, MOE_TOP_K)).reshape(-1)
    wt = weights.reshape(-1)
    n_assign = n_tok * MOE_TOP_K
    order = jnp.argsort(eid, stable=True)
    s_eid, s_tok, s_wt = eid[order], tok[order], wt[order]
    counts = jnp.bincount(eid, length=MOE_N_EXPERTS).astype(jnp.int32)
    start = jnp.cumsum(counts) - counts
    padded = ((counts + MOE_BLOCK - 1) // MOE_BLOCK) * MOE_BLOCK
    pend = jnp.cumsum(padded)
    pstart = pend - padded
    dest = pstart[s_eid] + (jnp.arange(n_assign, dtype=jnp.int32) - start[s_eid])
    n_rows = n_assign + MOE_N_EXPERTS * MOE_BLOCK
    n_blocks = n_rows // MOE_BLOCK
    row_tok = jnp.zeros((n_rows,), jnp.int32).at[dest].set(s_tok)
    row_wt = jnp.zeros((n_rows,), jnp.float32).at[dest].set(s_wt)
    blk_start = jnp.arange(n_blocks, dtype=jnp.int32) * MOE_BLOCK
    blk_eid = jnp.clip(jnp.searchsorted(pend, blk_start, side='right'), 0, MOE_N_EXPERTS - 1)

    def step(y, xs):
        e, toks, wts = xs
        hb = xf[toks]
        ob = (jax.nn.silu(hb @ w_gate[e]) * (hb @ w_up[e])) @ w_down[e]
        return y.at[toks].add(ob * wts[:, None].astype(ob.dtype)), None

    y, _ = lax.scan(step, jnp.zeros_like(xf),
                    (blk_eid, row_tok.reshape(n_blocks, MOE_BLOCK), row_wt.reshape(n_blocks, MOE_BLOCK)))
    return y.reshape(b, s, d)


def _normal(k, shape, scale):
    return jax.random.normal(k, shape, jnp.float32) * scale


def _gain(k, shape):
    return 1.0 + 0.05 * jax.random.normal(k, shape, jnp.float32)


def setup_inputs(seed: int = 0) -> dict:
    key = jax.random.key(seed)
    ks = jax.random.split(key, 21)
    D = D_MODEL
    return {
        "x": _normal(ks[0], (BATCH, SEQ, D), 1.0),
        "norm_mix": _gain(ks[1], (DEPTH, D)),
        "norm_ffn": _gain(ks[2], (DEPTH, D)),
        "gla_w_in": _normal(ks[3], (N_GLA_LAYERS, D, GLA_IN_WIDTH), D ** -0.5),
        "gla_w_gate_up": _normal(ks[4], (N_GLA_LAYERS, GLA_GATE_RANK, GLA_DK_TOTAL), GLA_GATE_RANK ** -0.5),
        "gla_b_gate": _normal(ks[5], (N_GLA_LAYERS, GLA_DK_TOTAL), 0.1),
        "gla_norm": _gain(ks[6], (N_GLA_LAYERS, GLA_DV)),
        "gla_w_out": _normal(ks[7], (N_GLA_LAYERS, GLA_DV_TOTAL, D), GLA_DV_TOTAL ** -0.5),
        "diff_w_in": _normal(ks[8], (N_DIFF_LAYERS, D, 3 * D), D ** -0.5),
        "diff_lambda_q1": _normal(ks[9], (N_DIFF_LAYERS, DIFF_HEAD_DIM), 0.1),
        "diff_lambda_k1": _normal(ks[10], (N_DIFF_LAYERS, DIFF_HEAD_DIM), 0.1),
        "diff_lambda_q2": _normal(ks[11], (N_DIFF_LAYERS, DIFF_HEAD_DIM), 0.1),
        "diff_lambda_k2": _normal(ks[12], (N_DIFF_LAYERS, DIFF_HEAD_DIM), 0.1),
        "diff_subln": _gain(ks[13], (N_DIFF_LAYERS, DIFF_V_DIM)),
        "diff_w_out": _normal(ks[14], (N_DIFF_LAYERS, D, D), D ** -0.5),
        "moe_w_group": _normal(ks[15], (DEPTH, D, MOE_GROUPS), D ** -0.5),
        "moe_w_expert": _normal(ks[16], (DEPTH, D, MOE_N_EXPERTS), D ** -0.5),
        "moe_w_gate": _normal(ks[17], (DEPTH, MOE_N_EXPERTS, D, MOE_D_FF), D ** -0.5),
        "moe_w_up": _normal(ks[18], (DEPTH, MOE_N_EXPERTS, D, MOE_D_FF), D ** -0.5),
        "moe_w_down": _normal(ks[19], (DEPTH, MOE_N_EXPERTS, MOE_D_FF, D), MOE_D_FF ** -0.5),
        "final_norm": _gain(ks[20], (D,)),
    }


def reference(x, norm_mix, norm_ffn, gla_w_in, gla_w_gate_up, gla_b_gate, gla_norm, gla_w_out,
              diff_w_in, diff_lambda_q1, diff_lambda_k1, diff_lambda_q2, diff_lambda_k2,
              diff_subln, diff_w_out, moe_w_group, moe_w_expert, moe_w_gate, moe_w_up,
              moe_w_down, final_norm):
    for i in range(DEPTH):
        hn = _rmsnorm(x, norm_mix[i])
        j = i // N_MIXERS
        if i % N_MIXERS == 0:
            x = x + _gla_mixer(hn, gla_w_in[j], gla_w_gate_up[j], gla_b_gate[j], gla_norm[j], gla_w_out[j])
        else:
            x = x + _diff_attention(hn, diff_w_in[j], diff_lambda_q1[j], diff_lambda_k1[j],
                                    diff_lambda_q2[j], diff_lambda_k2[j], diff_subln[j],
                                    diff_w_out[j], i)
        hn = _rmsnorm(x, norm_ffn[i])
        x = x + _hier_moe(hn, moe_w_group[i], moe_w_expert[i], moe_w_gate[i], moe_w_up[i], moe_w_down[i])
    return _rmsnorm(x, final_norm)
```

```python
import functools
import math

import jax
import jax.numpy as jnp
from jax import lax
from jax.experimental import pallas as pl
from jax.experimental.pallas import tpu as pltpu

F32 = jnp.float32
BF16 = jnp.bfloat16
I32 = jnp.int32

RMS_EPS = 1e-6
CHUNK = 64
GLA_HEADS = 4
GLA_GATE_TAU = 16.0
DIFF_HEADS = 8
MOE_GROUPS = 4
MOE_EXPERTS_PER_GROUP = 8
MOE_N_EXPERTS = MOE_GROUPS * MOE_EXPERTS_PER_GROUP

LANES = 128
SLAB = 256
VMEM_LIMIT = 56 * 1024 * 1024
LOG2E = 1.4426950408889634
NEG_BIG = -1e30

HIGHEST = lax.Precision.HIGHEST


def _cparams(sem):
    return pltpu.CompilerParams(dimension_semantics=sem, vmem_limit_bytes=VMEM_LIMIT)


def _norm_matmul_kernel(x_ref, g_ref, w_ref, cs_ref, *rest, n_slab, has_aux):
    if has_aux:
        w2_ref, o_ref, o2_ref, hn_ref = rest
    else:
        o_ref, hn_ref = rest

    @pl.when(pl.program_id(1) == 0)
    def _():
        x = x_ref[...]
        ms = jnp.mean(x * x, axis=-1, keepdims=True)
        hn = ((x * lax.rsqrt(ms + RMS_EPS)) * g_ref[...]).astype(BF16)
        hn_ref[...] = hn
        if has_aux:
            o2_ref[...] = jnp.dot(hn, w2_ref[...], preferred_element_type=F32)

    acc = jnp.dot(hn_ref[...], w_ref[...], preferred_element_type=F32) * cs_ref[...]
    for s in range(n_slab):
        o_ref[s] = acc[:, s * SLAB:(s + 1) * SLAB].astype(BF16)


def _norm_matmul(x, g, w, colscale, w_aux=None, *, tm=1024, tn=1024):
    t, d = x.shape
    n = w.shape[1]
    tm = min(tm, t)
    assert t % tm == 0 and n % tn == 0 and tn % SLAB == 0
    n_slab = tn // SLAB
    has_aux = w_aux is not None
    in_specs = [
        pl.BlockSpec((tm, d), lambda i, j: (i, 0)),
        pl.BlockSpec((1, d), lambda i, j: (0, 0)),
        pl.BlockSpec((d, tn), lambda i, j: (0, j)),
        pl.BlockSpec((1, tn), lambda i, j: (0, j)),
    ]
    out_shape = [jax.ShapeDtypeStruct((n // SLAB, t, SLAB), BF16)]
    out_specs = [pl.BlockSpec((n_slab, tm, SLAB), lambda i, j: (j, i, 0))]
    args = [x, g.reshape(1, d), w, colscale.reshape(1, n)]
    if has_aux:
        in_specs.append(pl.BlockSpec((d, LANES), lambda i, j: (0, 0)))
        out_shape.append(jax.ShapeDtypeStruct((t, LANES), F32))
        out_specs.append(pl.BlockSpec((tm, LANES), lambda i, j: (i, 0)))
        args.append(w_aux)
    res = pl.pallas_call(
        functools.partial(_norm_matmul_kernel, n_slab=n_slab, has_aux=has_aux),
        grid=(t // tm, n // tn),
        in_specs=in_specs,
        out_specs=out_specs,
        out_shape=out_shape,
        scratch_shapes=[pltpu.VMEM((tm, d), BF16)],
        compiler_params=_cparams(("parallel", "arbitrary")),
    )(*args)
    return res if has_aux else res[0]


def _gla_kernel(q_ref, k_ref, v_ref, r_ref, gz_ref, wg_ref, bg_ref, ng_ref, o_ref,
                state_ref, la_ref, ob_ref, *, n_chunks):
    @pl.when(pl.program_id(2) == 0)
    def _():
        state_ref[...] = jnp.zeros_like(state_ref)

    z = jnp.dot(gz_ref[...], wg_ref[...], preferred_element_type=F32,
                precision=HIGHEST) + bg_ref[...]
    la_ref[...] = -(jnp.maximum(-z, 0.0) + jnp.log1p(jnp.exp(-jnp.abs(z)))) * (1.0 / GLA_GATE_TAU)

    row = lax.broadcasted_iota(I32, (CHUNK, CHUNK), 0)
    col = lax.broadcasted_iota(I32, (CHUNK, CHUNK), 1)
    later = (col > row).astype(F32)
    ones = jnp.ones((CHUNK, LANES), F32)
    dv = state_ref.shape[1]

    def body(i, carry):
        sl = pl.ds(pl.multiple_of(i * CHUNK, CHUNK), CHUNK)
        la = la_ref[sl, :]
        suf = jnp.dot(later, la, preferred_element_type=F32, precision=HIGHEST)
        tot = lax.dot_general(la, ones, (((0,), (0,)), ((), ())),
                              preferred_element_type=F32, precision=HIGHEST)
        k_dec = (k_ref[0, sl, :].astype(F32) * jnp.exp(suf)).astype(BF16)
        v = jnp.concatenate([v_ref[0, sl, :], v_ref[1, sl, :]], axis=-1)
        upd = lax.dot_general(k_dec, v, (((0,), (0,)), ((), ())),
                              preferred_element_type=F32)
        decay = jnp.exp(tot)
        decay = jnp.concatenate([decay] * (dv // LANES), axis=-1)
        st = decay * state_ref[...] + upd
        state_ref[...] = st
        ob_ref[sl, :] = jnp.dot(q_ref[0, sl, :], st.astype(BF16), preferred_element_type=F32)
        return carry

    lax.fori_loop(0, n_chunks, body, 0)

    o = ob_ref[...]
    ms = jnp.mean(o * o, axis=-1, keepdims=True)
    on = (o * lax.rsqrt(ms + RMS_EPS)) * ng_ref[...]
    r = jnp.concatenate([r_ref[0], r_ref[1]], axis=-1).astype(F32)
    o_ref[...] = (on * (r * jax.nn.sigmoid(r))).astype(BF16)


def _gla_scan(proj, gz, w_gate_up, b_gate, norm_g, *, batch, seq, lc=512):
    t = batch * seq
    lc = min(lc, seq)
    assert seq % lc == 0 and lc % CHUNK == 0
    ncb = seq // lc
    dk = SLAB
    dv = 2 * SLAB
    d_out = GLA_HEADS * dv
    rank = w_gate_up.shape[0]
    wg = jnp.zeros((LANES, GLA_HEADS * dk), F32).at[:rank].set(w_gate_up)

    def tok(b, h, c):
        return b * ncb + c

    return pl.pallas_call(
        functools.partial(_gla_kernel, n_chunks=lc // CHUNK),
        grid=(batch, GLA_HEADS, ncb),
        in_specs=[
            pl.BlockSpec((1, lc, SLAB), lambda b, h, c: (h, tok(b, h, c), 0)),
            pl.BlockSpec((1, lc, SLAB), lambda b, h, c: (GLA_HEADS + h, tok(b, h, c), 0)),
            pl.BlockSpec((2, lc, SLAB), lambda b, h, c: (GLA_HEADS + h, tok(b, h, c), 0)),
            pl.BlockSpec((2, lc, SLAB), lambda b, h, c: (2 * GLA_HEADS + h, tok(b, h, c), 0)),
            pl.BlockSpec((lc, LANES), lambda b, h, c: (tok(b, h, c), 0)),
            pl.BlockSpec((LANES, dk), lambda b, h, c: (0, h)),
            pl.BlockSpec((1, dk), lambda b, h, c: (0, h)),
            pl.BlockSpec((1, dv), lambda b, h, c: (0, 0)),
        ],
        out_specs=pl.BlockSpec((lc, dv), lambda b, h, c: (tok(b, h, c), h)),
        out_shape=jax.ShapeDtypeStruct((t, d_out), BF16),
        scratch_shapes=[pltpu.VMEM((dk, dv), F32), pltpu.VMEM((lc, dk), F32),
                        pltpu.VMEM((lc, dv), F32)],
        compiler_params=_cparams(("parallel", "parallel", "arbitrary")),
    )(proj, proj, proj, proj, gz, wg, b_gate.reshape(1, -1), norm_g.reshape(1, -1))


def _diff_attn_kernel(slope_ref, q_ref, k_ref, v_ref, lq1_ref, lk1_ref, lq2_ref, lk2_ref,
                      sg_ref, o_ref, m_ref, l_ref, acc_ref, *, tq, lambda_init):
    h = pl.program_id(1)
    qi = pl.program_id(2)
    hd = SLAB // 2
    slope2 = slope_ref[h] * LOG2E
    q0 = qi * tq
    q = q_ref[0]
    qs = (q[:, :hd], q[:, hd:])

    m_ref[...] = jnp.full_like(m_ref, NEG_BIG)
    l_ref[...] = jnp.zeros_like(l_ref)
    acc_ref[...] = jnp.zeros_like(acc_ref)

    def update(j, bias):
        start = pl.multiple_of(j * tq, tq)
        k = k_ref[0, pl.ds(start, tq), :]
        v = v_ref[0, pl.ds(start, tq), :]
        for u in range(2):
            s = lax.dot_general(qs[u], k[:, u * hd:(u + 1) * hd], (((1,), (1,)), ((), ())),
                                preferred_element_type=F32) + bias
            m_prev = m_ref[u]
            m_new = jnp.maximum(m_prev, jnp.max(s, axis=-1, keepdims=True))
            alpha = jnp.exp2(m_prev - m_new)
            p = jnp.exp2(s - m_new)
            l_ref[u] = alpha * l_ref[u] + jnp.sum(p, axis=-1, keepdims=True)
            acc_ref[u] = alpha * acc_ref[u] + jnp.dot(p.astype(BF16), v,
                                                      preferred_element_type=F32)
            m_ref[u] = m_new

    kcol = lax.broadcasted_iota(I32, (1, tq), 1)

    def past_block(j, carry):
        bias = slope2 * (kcol + (j * tq - q0)).astype(F32)
        update(j, bias)
        return carry

    lax.fori_loop(0, qi, past_block, 0)

    qrow = lax.broadcasted_iota(I32, (tq, tq), 0)
    kc = lax.broadcasted_iota(I32, (tq, tq), 1)
    dist = jnp.abs(qrow - kc)
    bias = slope2 * (qrow - dist).astype(F32)
    allowed = (qrow // CHUNK) >= (kc // CHUNK)
    update(qi, jnp.where(allowed, bias, -jnp.inf))

    lam = (jnp.exp(jnp.sum(lq1_ref[...] * lk1_ref[...], axis=-1, keepdims=True))
           - jnp.exp(jnp.sum(lq2_ref[...] * lk2_ref[...], axis=-1, keepdims=True))
           + lambda_init)
    o = acc_ref[0] / l_ref[0] - lam * (acc_ref[1] / l_ref[1])
    ms = jnp.mean(o * o, axis=-1, keepdims=True)
    o = (o * lax.rsqrt(ms + RMS_EPS)) * sg_ref[...]
    o_ref[...] = (o * (1.0 - lambda_init)).astype(BF16)


def _diff_attention(qkv, lq1, lk1, lq2, lk2, subln, *, batch, seq, layer_idx, tq=512):
    t = batch * seq
    tq = min(tq, seq)
    assert seq % tq == 0 and tq % CHUNK == 0
    nq = seq // tq
    lambda_init = 0.8 - 0.6 * math.exp(-0.3 * layer_idx)
    slopes = 2.0 ** (-8.0 * jnp.arange(1, DIFF_HEADS + 1, dtype=F32) / DIFF_HEADS)
    vec = lambda a: a.reshape(1, -1).astype(F32)
    grid_spec = pltpu.PrefetchScalarGridSpec(
        num_scalar_prefetch=1,
        grid=(batch, DIFF_HEADS, nq),
        in_specs=[
            pl.BlockSpec((1, tq, SLAB), lambda b, h, i, s: (h, b * nq + i, 0)),
            pl.BlockSpec((1, seq, SLAB), lambda b, h, i, s: (DIFF_HEADS + h, b, 0)),
            pl.BlockSpec((1, seq, SLAB), lambda b, h, i, s: (2 * DIFF_HEADS + h, b, 0)),
            pl.BlockSpec((1, SLAB // 2), lambda b, h, i, s: (0, 0)),
            pl.BlockSpec((1, SLAB // 2), lambda b, h, i, s: (0, 0)),
            pl.BlockSpec((1, SLAB // 2), lambda b, h, i, s: (0, 0)),
            pl.BlockSpec((1, SLAB // 2), lambda b, h, i, s: (0, 0)),
            pl.BlockSpec((1, SLAB), lambda b, h, i, s: (0, 0)),
        ],
        out_specs=pl.BlockSpec((tq, SLAB), lambda b, h, i, s: (b * nq + i, h)),
        scratch_shapes=[pltpu.VMEM((2, tq, 1), F32), pltpu.VMEM((2, tq, 1), F32),
                        pltpu.VMEM((2, tq, SLAB), F32)],
    )
    return pl.pallas_call(
        functools.partial(_diff_attn_kernel, tq=tq, lambda_init=lambda_init),
        grid_spec=grid_spec,
        out_shape=jax.ShapeDtypeStruct((t, DIFF_HEADS * SLAB), BF16),
        compiler_params=_cparams(("parallel", "parallel", "arbitrary")),
    )(slopes, qkv, qkv, qkv, vec(lq1), vec(lk1), vec(lq2), vec(lk2), vec(subln))


def _proj_router_kernel(o_ref, w_ref, x_ref, g_ref, wr_ref, x1_ref, ri_ref, rw_ref):
    x1 = x_ref[...] + jnp.dot(o_ref[...], w_ref[...], preferred_element_type=F32)
    x1_ref[...] = x1
    ms = jnp.mean(x1 * x1, axis=-1, keepdims=True)
    hn = ((x1 * lax.rsqrt(ms + RMS_EPS)) * g_ref[...]).astype(BF16)
    logits = jnp.dot(hn, wr_ref[...], preferred_element_type=F32)
    lane = lax.broadcasted_iota(I32, logits.shape, 1)
    neg = -jnp.inf

    def first_max(vals):
        m = jnp.max(vals, axis=-1, keepdims=True)
        idx = jnp.min(jnp.where(vals == m, lane, LANES), axis=-1, keepdims=True)
        return m, idx

    gmask = lane < MOE_GROUPS
    gmax, gidx = first_max(jnp.where(gmask, logits, neg))
    gden = jnp.sum(jnp.where(gmask, jnp.exp(logits - gmax), 0.0), axis=-1, keepdims=True)
    g_w = 1.0 / gden
    lo = MOE_GROUPS + MOE_EXPERTS_PER_GROUP * gidx
    emask = (lane >= lo) & (lane < lo + MOE_EXPERTS_PER_GROUP)
    elog = jnp.where(emask, logits, neg)
    m1, i1 = first_max(elog)
    m2, i2 = first_max(jnp.where(lane == i1, neg, elog))
    tt = jnp.exp(m2 - m1)
    w1 = g_w / (1.0 + tt)
    w2 = w1 * tt
    ri_ref[...] = jnp.where(lane == 0, i1 - MOE_GROUPS, jnp.where(lane == 1, i2 - MOE_GROUPS, 0))
    rw_ref[...] = jnp.where(lane == 0, w1, jnp.where(lane == 1, w2, 0.0))


def _proj_router(o, w_out, x, g_ffn, w_router, *, tm=512):
    t, d = x.shape
    tm = min(tm, t)
    assert t % tm == 0
    return pl.pallas_call(
        _proj_router_kernel,
        grid=(t // tm,),
        in_specs=[
            pl.BlockSpec((tm, d), lambda i: (i, 0)),
            pl.BlockSpec((d, d), lambda i: (0, 0)),
            pl.BlockSpec((tm, d), lambda i: (i, 0)),
            pl.BlockSpec((1, d), lambda i: (0, 0)),
            pl.BlockSpec((d, LANES), lambda i: (0, 0)),
        ],
        out_specs=[
            pl.BlockSpec((tm, d), lambda i: (i, 0)),
            pl.BlockSpec((tm, LANES), lambda i: (i, 0)),
            pl.BlockSpec((tm, LANES), lambda i: (i, 0)),
        ],
        out_shape=[jax.ShapeDtypeStruct((t, d), F32), jax.ShapeDtypeStruct((t, LANES), I32),
                   jax.ShapeDtypeStruct((t, LANES), F32)],
        compiler_params=_cparams(("parallel",)),
    )(o, w_out, x, g_ffn.reshape(1, d), w_router)


def _gather_rows(idx_ref, n, src_hbm, dst_ref, sem):
    def body(r, carry):
        t = idx_ref[0, 0, r]
        pltpu.make_async_copy(src_hbm.at[pl.ds(t, 1), :], dst_ref.at[pl.ds(r, 1), :], sem).start()
        return carry
    lax.fori_loop(0, n, body, 0, unroll=8)


def _wait_rows(n, src_hbm, dst_ref, sem):
    pltpu.make_async_copy(src_hbm.at[pl.ds(0, n), :], dst_ref, sem).wait()


def _moe_kernel(eid_ref, nused_ref, tok_ref, tok_next_ref, x_hbm, g_ref, wg_ref, wu_ref, wd_ref,
                o_ref, xbuf, sem, *, blk):
    b = pl.program_id(0)
    nb = pl.num_programs(0)
    slot = b % 2

    @pl.when(b == 0)
    def _():
        _gather_rows(tok_ref, blk, x_hbm, xbuf.at[0], sem.at[0])

    @pl.when(b + 1 < nb)
    def _():
        _gather_rows(tok_next_ref, blk, x_hbm, xbuf.at[1 - slot], sem.at[1 - slot])

    _wait_rows(blk, x_hbm, xbuf.at[slot], sem.at[slot])

    @pl.when(b < nused_ref[0])
    def _():
        x = xbuf[slot]
        ms = jnp.mean(x * x, axis=-1, keepdims=True)
        hn = ((x * lax.rsqrt(ms + RMS_EPS)) * g_ref[...]).astype(BF16)
        hg = jnp.dot(hn, wg_ref[0], preferred_element_type=F32)
        hu = jnp.dot(hn, wu_ref[0], preferred_element_type=F32)
        act = ((hg * jax.nn.sigmoid(hg)) * hu).astype(BF16)
        o_ref[...] = jnp.dot(act, wd_ref[0], preferred_element_type=F32)

    @pl.when(b >= nused_ref[0])
    def _():
        o_ref[...] = jnp.zeros_like(o_ref)


def _moe_ffn(x, g_ffn, blk_eid, n_used, row_tok, w_gate, w_up, w_down, *, blk):
    t, d = x.shape
    n_rows = row_tok.shape[0]
    n_blocks = n_rows // blk
    dff = w_gate.shape[-1]
    tok3 = row_tok.reshape(n_blocks, 1, blk)
    grid_spec = pltpu.PrefetchScalarGridSpec(
        num_scalar_prefetch=2,
        grid=(n_blocks,),
        in_specs=[
            pl.BlockSpec((1, 1, blk), lambda b, e, u: (b, 0, 0), memory_space=pltpu.SMEM),
            pl.BlockSpec((1, 1, blk), lambda b, e, u: (jnp.minimum(b + 1, n_blocks - 1), 0, 0),
                         memory_space=pltpu.SMEM),
            pl.BlockSpec(memory_space=pl.ANY),
            pl.BlockSpec((1, d), lambda b, e, u: (0, 0)),
            pl.BlockSpec((1, d, dff), lambda b, e, u: (e[b], 0, 0)),
            pl.BlockSpec((1, d, dff), lambda b, e, u: (e[b], 0, 0)),
            pl.BlockSpec((1, dff, d), lambda b, e, u: (e[b], 0, 0)),
        ],
        out_specs=pl.BlockSpec((blk, d), lambda b, e, u: (b, 0)),
        scratch_shapes=[pltpu.VMEM((2, blk, d), F32), pltpu.SemaphoreType.DMA((2,))],
    )
    return pl.pallas_call(
        functools.partial(_moe_kernel, blk=blk),
        grid_spec=grid_spec,
        out_shape=jax.ShapeDtypeStruct((n_rows, d), F32),
        compiler_params=_cparams(("arbitrary",)),
    )(blk_eid, n_used, tok3, tok3, x, g_ffn.reshape(1, d), w_gate, w_up, w_down)


def _combine_kernel(pos_ref, pos_next_ref, x_ref, rw_ref, y_hbm, g_ref, o_ref, ybuf, sem,
                    *, tc, final_norm):
    i = pl.program_id(0)
    n = pl.num_programs(0)
    slot = i % 2

    @pl.when(i == 0)
    def _():
        _gather_rows(pos_ref, 2 * tc, y_hbm, ybuf.at[0], sem.at[0])

    @pl.when(i + 1 < n)
    def _():
        _gather_rows(pos_next_ref, 2 * tc, y_hbm, ybuf.at[1 - slot], sem.at[1 - slot])

    _wait_rows(2 * tc, y_hbm, ybuf.at[slot], sem.at[slot])

    rw = rw_ref[...]
    y = x_ref[...] + rw[:, 0:1] * ybuf[slot, pl.ds(0, tc), :] + rw[:, 1:2] * ybuf[slot, pl.ds(tc, tc), :]
    if final_norm:
        ms = jnp.mean(y * y, axis=-1, keepdims=True)
        y = (y * lax.rsqrt(ms + RMS_EPS)) * g_ref[...]
    o_ref[...] = y


def _combine(x, route_w, pos, y_sorted, g_final, *, final_norm, tc=256):
    t, d = x.shape
    tc = min(tc, t)
    assert t % tc == 0
    nt = t // tc
    pos3 = pos.reshape(nt, tc, 2).transpose(0, 2, 1).reshape(nt, 1, 2 * tc)
    return pl.pallas_call(
        functools.partial(_combine_kernel, tc=tc, final_norm=final_norm),
        grid=(nt,),
        in_specs=[
            pl.BlockSpec((1, 1, 2 * tc), lambda i: (i, 0, 0), memory_space=pltpu.SMEM),
            pl.BlockSpec((1, 1, 2 * tc), lambda i: (jnp.minimum(i + 1, nt - 1), 0, 0),
                         memory_space=pltpu.SMEM),
            pl.BlockSpec((tc, d), lambda i: (i, 0)),
            pl.BlockSpec((tc, LANES), lambda i: (i, 0)),
            pl.BlockSpec(memory_space=pl.ANY),
            pl.BlockSpec((1, d), lambda i: (0, 0)),
        ],
        out_specs=pl.BlockSpec((tc, d), lambda i: (i, 0)),
        out_shape=jax.ShapeDtypeStruct((t, d), F32),
        scratch_shapes=[pltpu.VMEM((2, 2 * tc, d), F32), pltpu.SemaphoreType.DMA((2,))],
        compiler_params=_cparams(("arbitrary",)),
    )(pos3, pos3, x, route_w, y_sorted, g_final.reshape(1, d))


def _dispatch(route_i, blk):
    t = route_i.shape[0]
    eid = route_i[:, :2].reshape(-1)
    n_assign = eid.shape[0]
    onehot = (eid[:, None] == jnp.arange(MOE_N_EXPERTS, dtype=I32)[None, :]).astype(I32)
    csum = jnp.cumsum(onehot, axis=0)
    rank = jnp.sum(csum * onehot, axis=1) - 1
    counts = csum[-1]
    padded = ((counts + blk - 1) // blk) * blk
    pend = jnp.cumsum(padded)
    pstart = pend - padded
    dest = (pstart[eid] + rank).astype(I32)
    n_rows = n_assign + MOE_N_EXPERTS * blk
    n_blocks = n_rows // blk
    row_tok = jnp.zeros((n_rows,), I32).at[dest].set(jnp.arange(n_assign, dtype=I32) // 2)
    blk_start = jnp.arange(n_blocks, dtype=I32) * blk
    blk_eid = jnp.clip(jnp.searchsorted(pend, blk_start, side='right'), 0,
                       MOE_N_EXPERTS - 1).astype(I32)
    n_used = (pend[-1:] // blk).astype(I32)
    return row_tok, blk_eid, n_used, dest.reshape(t, 2)


def _router_weights(w_group, w_expert):
    d = w_group.shape[0]
    w = jnp.zeros((d, LANES), F32)
    w = w.at[:, :MOE_GROUPS].set(w_group).at[:, MOE_GROUPS:MOE_GROUPS + MOE_N_EXPERTS].set(w_expert)
    return w.astype(BF16)


def _moe_layer(x1, route_i, route_w, g_ffn, w_gate, w_up, w_down, g_final, *, final_norm, blk=256):
    row_tok, blk_eid, n_used, pos = _dispatch(route_i, blk)
    y_sorted = _moe_ffn(x1, g_ffn, blk_eid, n_used, row_tok, w_gate.astype(BF16),
                        w_up.astype(BF16), w_down.astype(BF16), blk=blk)
    return _combine(x1, route_w, pos, y_sorted, g_final, final_norm=final_norm)


def kernel(x, norm_mix, norm_ffn, gla_w_in, gla_w_gate_up, gla_b_gate, gla_norm, gla_w_out,
           diff_w_in, diff_lambda_q1, diff_lambda_k1, diff_lambda_q2, diff_lambda_k2,
           diff_subln, diff_w_out, moe_w_group, moe_w_expert, moe_w_gate, moe_w_up,
           moe_w_down, final_norm):
    batch, seq, d = x.shape
    t = batch * seq
    xf = x.reshape(t, d)

    dk_total = d // 2
    n_main = 2 * dk_total + 2 * d
    w_in = gla_w_in[0]
    gla_dk = dk_total // GLA_HEADS
    colscale = jnp.ones((n_main,), F32).at[:dk_total].set(gla_dk ** -0.5)
    rank = w_in.shape[1] - n_main
    w_gz = jnp.zeros((d, LANES), F32).at[:, :rank].set(w_in[:, n_main:]).astype(BF16)
    proj, gz = _norm_matmul(xf, norm_mix[0], w_in[:, :n_main].astype(BF16), colscale, w_gz)
    o = _gla_scan(proj, gz, gla_w_gate_up[0], gla_b_gate[0], gla_norm[0], batch=batch, seq=seq)
    x1, ri, rw = _proj_router(o, gla_w_out[0].astype(BF16), xf, norm_ffn[0],
                              _router_weights(moe_w_group[0], moe_w_expert[0]))
    x2 = _moe_layer(x1, ri, rw, norm_ffn[0], moe_w_gate[0], moe_w_up[0], moe_w_down[0],
                    final_norm, final_norm=False)

    hd = d // DIFF_HEADS // 2
    colscale = jnp.ones((3 * d,), F32).at[:d].set(hd ** -0.5 * LOG2E)
    qkv = _norm_matmul(x2, norm_mix[1], diff_w_in[0].astype(BF16), colscale)
    o = _diff_attention(qkv, diff_lambda_q1[0], diff_lambda_k1[0], diff_lambda_q2[0],
                        diff_lambda_k2[0], diff_subln[0], batch=batch, seq=seq, layer_idx=1)
    x3, ri, rw = _proj_router(o, diff_w_out[0].astype(BF16), x2, norm_ffn[1],
                              _router_weights(moe_w_group[1], moe_w_expert[1]))
    out = _moe_layer(x3, ri, rw, norm_ffn[1], moe_w_gate[1], moe_w_up[1], moe_w_down[1],
                     final_norm, final_norm=True)
    return out.reshape(batch, seq, d)
```

```python
import functools
import math

import jax
import jax.numpy as jnp
from jax import lax
from jax.experimental import pallas as pl
from jax.experimental.pallas import tpu as pltpu

F32 = jnp.float32
BF16 = jnp.bfloat16
I32 = jnp.int32

RMS_EPS = 1e-6
CHUNK = 64
GLA_HEADS = 4
GLA_GATE_TAU = 16.0
DIFF_HEADS = 8
MOE_GROUPS = 4
MOE_EXPERTS_PER_GROUP = 8
MOE_N_EXPERTS = MOE_GROUPS * MOE_EXPERTS_PER_GROUP

LANES = 128
SLAB = 256
VMEM_LIMIT = 56 * 1024 * 1024
LOG2E = 1.4426950408889634
NEG_BIG = -1e30

HIGHEST = lax.Precision.HIGHEST


def _cparams(sem):
    return pltpu.CompilerParams(dimension_semantics=sem, vmem_limit_bytes=VMEM_LIMIT)


def _norm_matmul_kernel(x_ref, g_ref, w_ref, cs_ref, *rest, n_slab, has_aux):
    if has_aux:
        w2_ref, o_ref, o2_ref, hn_ref = rest
    else:
        o_ref, hn_ref = rest

    @pl.when(pl.program_id(1) == 0)
    def _():
        x = x_ref[...]
        ms = jnp.mean(x * x, axis=-1, keepdims=True)
        hn = ((x * lax.rsqrt(ms + RMS_EPS)) * g_ref[...]).astype(BF16)
        hn_ref[...] = hn
        if has_aux:
            o2_ref[...] = jnp.dot(hn, w2_ref[...], preferred_element_type=F32)

    acc = jnp.dot(hn_ref[...], w_ref[...], preferred_element_type=F32) * cs_ref[...]
    for s in range(n_slab):
        o_ref[s] = acc[:, s * SLAB:(s + 1) * SLAB].astype(BF16)


def _norm_matmul(x, g, w, colscale, w_aux=None, *, tm=1024, tn=1024):
    t, d = x.shape
    n = w.shape[1]
    tm = min(tm, t)
    assert t % tm == 0 and n % tn == 0 and tn % SLAB == 0
    n_slab = tn // SLAB
    has_aux = w_aux is not None
    in_specs = [
        pl.BlockSpec((tm, d), lambda i, j: (i, 0)),
        pl.BlockSpec((1, d), lambda i, j: (0, 0)),
        pl.BlockSpec((d, tn), lambda i, j: (0, j)),
        pl.BlockSpec((1, tn), lambda i, j: (0, j)),
    ]
    out_shape = [jax.ShapeDtypeStruct((n // SLAB, t, SLAB), BF16)]
    out_specs = [pl.BlockSpec((n_slab, tm, SLAB), lambda i, j: (j, i, 0))]
    args = [x, g.reshape(1, d), w, colscale.reshape(1, n)]
    if has_aux:
        in_specs.append(pl.BlockSpec((d, LANES), lambda i, j: (0, 0)))
        out_shape.append(jax.ShapeDtypeStruct((t, LANES), F32))
        out_specs.append(pl.BlockSpec((tm, LANES), lambda i, j: (i, 0)))
        args.append(w_aux)
    res = pl.pallas_call(
        functools.partial(_norm_matmul_kernel, n_slab=n_slab, has_aux=has_aux),
        grid=(t // tm, n // tn),
        in_specs=in_specs,
        out_specs=out_specs,
        out_shape=out_shape,
        scratch_shapes=[pltpu.VMEM((tm, d), BF16)],
        compiler_params=_cparams(("parallel", "arbitrary")),
    )(*args)
    return res if has_aux else res[0]


def _gla_kernel(q_ref, k_ref, v_ref, r_ref, gz_ref, wg_ref, bg_ref, ng_ref, o_ref,
                state_ref, la_ref, ob_ref, *, n_chunks):
    @pl.when(pl.program_id(2) == 0)
    def _():
        state_ref[...] = jnp.zeros_like(state_ref)

    z = jnp.dot(gz_ref[...], wg_ref[...], preferred_element_type=F32,
                precision=HIGHEST) + bg_ref[...]
    la_ref[...] = -(jnp.maximum(-z, 0.0) + jnp.log1p(jnp.exp(-jnp.abs(z)))) * (1.0 / GLA_GATE_TAU)

    row = lax.broadcasted_iota(I32, (CHUNK, CHUNK), 0)
    col = lax.broadcasted_iota(I32, (CHUNK, CHUNK), 1)
    later = (col > row).astype(F32)
    ones = jnp.ones((CHUNK, LANES), F32)
    dv = state_ref.shape[1]

    def body(i, carry):
        sl = pl.ds(pl.multiple_of(i * CHUNK, CHUNK), CHUNK)
        la = la_ref[sl, :]
        suf = jnp.dot(later, la, preferred_element_type=F32, precision=HIGHEST)
        tot = lax.dot_general(la, ones, (((0,), (0,)), ((), ())),
                              preferred_element_type=F32, precision=HIGHEST)
        k_dec = (k_ref[0, sl, :].astype(F32) * jnp.exp(suf)).astype(BF16)
        v = jnp.concatenate([v_ref[0, sl, :], v_ref[1, sl, :]], axis=-1)
        upd = lax.dot_general(k_dec, v, (((0,), (0,)), ((), ())),
                              preferred_element_type=F32)
        decay = jnp.exp(tot)
        decay = jnp.concatenate([decay] * (dv // LANES), axis=-1)
        st = decay * state_ref[...] + upd
        state_ref[...] = st
        ob_ref[sl, :] = jnp.dot(q_ref[0, sl, :], st.astype(BF16), preferred_element_type=F32)
        return carry

    lax.fori_loop(0, n_chunks, body, 0)

    o = ob_ref[...]
    ms = jnp.mean(o * o, axis=-1, keepdims=True)
    on = (o * lax.rsqrt(ms + RMS_EPS)) * ng_ref[...]
    r = jnp.concatenate([r_ref[0], r_ref[1]], axis=-1).astype(F32)
    o_ref[...] = (on * (r * jax.nn.sigmoid(r))).astype(BF16)


def _gla_scan(proj, gz, w_gate_up, b_gate, norm_g, *, batch, seq, lc=512):
    t = batch * seq
    lc = min(lc, seq)
    assert seq % lc == 0 and lc % CHUNK == 0
    ncb = seq // lc
    dk = SLAB
    dv = 2 * SLAB
    d_out = GLA_HEADS * dv
    rank = w_gate_up.shape[0]
    wg = jnp.zeros((LANES, GLA_HEADS * dk), F32).at[:rank].set(w_gate_up)

    def tok(b, h, c):
        return b * ncb + c

    return pl.pallas_call(
        functools.partial(_gla_kernel, n_chunks=lc // CHUNK),
        grid=(batch, GLA_HEADS, ncb),
        in_specs=[
            pl.BlockSpec((1, lc, SLAB), lambda b, h, c: (h, tok(b, h, c), 0)),
            pl.BlockSpec((1, lc, SLAB), lambda b, h, c: (GLA_HEADS + h, tok(b, h, c), 0)),
            pl.BlockSpec((2, lc, SLAB), lambda b, h, c: (GLA_HEADS + h, tok(b, h, c), 0)),
            pl.BlockSpec((2, lc, SLAB), lambda b, h, c: (2 * GLA_HEADS + h, tok(b, h, c), 0)),
            pl.BlockSpec((lc, LANES), lambda b, h, c: (tok(b, h, c), 0)),
            pl.BlockSpec((LANES, dk), lambda b, h, c: (0, h)),
            pl.BlockSpec((1, dk), lambda b, h, c: (0, h)),
            pl.BlockSpec((1, dv), lambda b, h, c: (0, 0)),
        ],
        out_specs=pl.BlockSpec((lc, dv), lambda b, h, c: (tok(b, h, c), h)),
        out_shape=jax.ShapeDtypeStruct((t, d_out), BF16),
        scratch_shapes=[pltpu.VMEM((dk, dv), F32), pltpu.VMEM((lc, dk), F32),
                        pltpu.VMEM((lc, dv), F32)],
        compiler_params=_cparams(("parallel", "parallel", "arbitrary")),
    )(proj, proj, proj, proj, gz, wg, b_gate.reshape(1, -1), norm_g.reshape(1, -1))


def _diff_attn_kernel(slope_ref, q_ref, k_ref, v_ref, qfeat_ref, kfeat_ref, lq1_ref, lk1_ref,
                      lq2_ref, lk2_ref, sg_ref, o_ref, vt_ref, dbias_ref, r0_ref, m_ref, l_ref, acc_ref,
                      *, tq, lambda_init):
    h = pl.program_id(1)
    qi = pl.program_id(2)
    hd = SLAB // 2
    c = slope_ref[h] * LOG2E
    nt = (((1,), (1,)), ((), ()))

    @pl.when(qi == 0)
    def _():
        def transpose_block(i, carry):
            sl = pl.ds(pl.multiple_of(i * tq, tq), tq)
            vt_ref[:, sl] = v_ref[0, sl, :].T
            return carry
        lax.fori_loop(0, v_ref.shape[1] // tq, transpose_block, 0)
        ki = lax.broadcasted_iota(I32, (tq, tq), 0)
        qj = lax.broadcasted_iota(I32, (tq, tq), 1)
        bias = c * (qj - jnp.abs(qj - ki)).astype(F32)
        dbias_ref[...] = jnp.where((qj // CHUNK) >= (ki // CHUNK), bias, -jnp.inf)

    q = q_ref[0]
    start = pl.multiple_of(qi * tq, tq)

    k = k_ref[0, pl.ds(start, tq), :]
    vt = vt_ref[:, pl.ds(start, tq)]
    for u in range(2):
        r = lax.dot_general(k[:, u * hd:(u + 1) * hd], q[:, u * hd:(u + 1) * hd], nt,
                            preferred_element_type=F32) + dbias_ref[...]
        m_new = jnp.max(r, axis=0, keepdims=True)
        p = jnp.exp2(r - m_new)
        l_ref[u] = jnp.sum(p, axis=0, keepdims=True)
        acc_ref[u] = jnp.dot(vt, p.astype(BF16), preferred_element_type=F32)
        m_ref[u] = m_new

    qfeat = jnp.broadcast_to(qfeat_ref[0], (tq, LANES))
    kfeat = kfeat_ref[...]
    qa = [jnp.concatenate([q[:, u * hd:(u + 1) * hd], qfeat], axis=-1) for u in range(2)]

    def scores(u, blk):
        st = pl.multiple_of(blk * tq, tq)
        ka = jnp.concatenate([k_ref[0, pl.ds(st, tq), u * hd:(u + 1) * hd], kfeat], axis=-1)
        return lax.dot_general(ka, qa[u], nt, preferred_element_type=F32)

    def accumulate(u, r, vt, off):
        m_prev = m_ref[u]
        m_new = jnp.maximum(m_prev, jnp.max(r, axis=0, keepdims=True) - off)
        alpha = jnp.exp2(m_prev - m_new)
        p = jnp.exp2(r - (m_new + off))
        l_ref[u] = alpha * l_ref[u] + jnp.sum(p, axis=0, keepdims=True)
        acc_ref[u] = alpha * acc_ref[u] + jnp.dot(vt, p.astype(BF16),
                                                  preferred_element_type=F32)
        m_ref[u] = m_new

    r0_ref[...] = scores(0, jnp.maximum(qi - 1, 0))

    def past_block(dd, carry):
        blk = qi - dd
        vt = vt_ref[:, pl.ds(pl.multiple_of(blk * tq, tq), tq)]
        off = c * (dd * tq).astype(F32)
        r1 = scores(1, blk)
        accumulate(0, r0_ref[...], vt, off)
        r0_ref[...] = scores(0, jnp.maximum(blk - 1, 0))
        accumulate(1, r1, vt, off)
        return carry

    lax.fori_loop(1, qi + 1, past_block, 0)

    lam = (jnp.exp(jnp.sum(lq1_ref[...] * lk1_ref[...], axis=-1, keepdims=True))
           - jnp.exp(jnp.sum(lq2_ref[...] * lk2_ref[...], axis=-1, keepdims=True))
           + lambda_init)
    o = acc_ref[0] * (1.0 / l_ref[0]) - lam * (acc_ref[1] * (1.0 / l_ref[1]))
    o = o.T
    ms = jnp.mean(o * o, axis=-1, keepdims=True)
    o = (o * lax.rsqrt(ms + RMS_EPS)) * sg_ref[...]
    o_ref[...] = (o * (1.0 - lambda_init)).astype(BF16)


def _alibi_features(slopes, tq):
    c = slopes * LOG2E
    c_hi = c.astype(BF16).astype(F32)
    c_lo = c - c_hi
    qf = jnp.zeros((slopes.shape[0], 1, LANES), F32)
    qf = qf.at[:, 0, 0].set(LANES * c_hi).at[:, 0, 1].set(c_hi)
    qf = qf.at[:, 0, 2].set(LANES * c_lo).at[:, 0, 3].set(c_lo)
    idx = jnp.arange(tq, dtype=I32)
    hi = (idx // LANES).astype(F32)
    lo = (idx % LANES).astype(F32)
    kf = jnp.zeros((tq, LANES), F32).at[:, 0].set(hi).at[:, 1].set(lo).at[:, 2].set(hi).at[:, 3].set(lo)
    return qf.astype(BF16), kf.astype(BF16)


def _diff_attention(qkv, lq1, lk1, lq2, lk2, subln, *, batch, seq, layer_idx, tq=512):
    t = batch * seq
    tq = min(tq, seq)
    assert seq % tq == 0 and tq % CHUNK == 0
    nq = seq // tq
    lambda_init = 0.8 - 0.6 * math.exp(-0.3 * layer_idx)
    slopes = 2.0 ** (-8.0 * jnp.arange(1, DIFF_HEADS + 1, dtype=F32) / DIFF_HEADS)
    vec = lambda a: a.reshape(1, -1).astype(F32)
    qfeat, kfeat = _alibi_features(slopes, tq)
    grid_spec = pltpu.PrefetchScalarGridSpec(
        num_scalar_prefetch=1,
        grid=(batch, DIFF_HEADS, nq),
        in_specs=[
            pl.BlockSpec((1, tq, SLAB), lambda b, h, i, s: (h, b * nq + i, 0)),
            pl.BlockSpec((1, seq, SLAB), lambda b, h, i, s: (DIFF_HEADS + h, b, 0)),
            pl.BlockSpec((1, seq, SLAB), lambda b, h, i, s: (2 * DIFF_HEADS + h, b, 0)),
            pl.BlockSpec((1, 1, LANES), lambda b, h, i, s: (h, 0, 0)),
            pl.BlockSpec((tq, LANES), lambda b, h, i, s: (0, 0)),
            pl.BlockSpec((1, SLAB // 2), lambda b, h, i, s: (0, 0)),
            pl.BlockSpec((1, SLAB // 2), lambda b, h, i, s: (0, 0)),
            pl.BlockSpec((1, SLAB // 2), lambda b, h, i, s: (0, 0)),
            pl.BlockSpec((1, SLAB // 2), lambda b, h, i, s: (0, 0)),
            pl.BlockSpec((1, SLAB), lambda b, h, i, s: (0, 0)),
        ],
        out_specs=pl.BlockSpec((tq, SLAB), lambda b, h, i, s: (b * nq + i, h)),
        scratch_shapes=[pltpu.VMEM((SLAB, seq), BF16), pltpu.VMEM((tq, tq), F32),
                        pltpu.VMEM((tq, tq), F32),
                        pltpu.VMEM((2, 1, tq), F32), pltpu.VMEM((2, 1, tq), F32),
                        pltpu.VMEM((2, SLAB, tq), F32)],
    )
    return pl.pallas_call(
        functools.partial(_diff_attn_kernel, tq=tq, lambda_init=lambda_init),
        grid_spec=grid_spec,
        out_shape=jax.ShapeDtypeStruct((t, DIFF_HEADS * SLAB), BF16),
        compiler_params=_cparams(("parallel", "parallel", "arbitrary")),
    )(slopes, qkv, qkv, qkv, qfeat, kfeat, vec(lq1), vec(lk1), vec(lq2), vec(lk2), vec(subln))


def _proj_router_kernel(o_ref, w_ref, x_ref, g_ref, wr_ref, x1_ref, ri_ref, rw_ref):
    x1 = x_ref[...] + jnp.dot(o_ref[...], w_ref[...], preferred_element_type=F32)
    x1_ref[...] = x1
    ms = jnp.mean(x1 * x1, axis=-1, keepdims=True)
    hn = ((x1 * lax.rsqrt(ms + RMS_EPS)) * g_ref[...]).astype(BF16)
    logits = jnp.dot(hn, wr_ref[...], preferred_element_type=F32)
    lane = lax.broadcasted_iota(I32, logits.shape, 1)
    neg = -jnp.inf

    def first_max(vals):
        m = jnp.max(vals, axis=-1, keepdims=True)
        idx = jnp.min(jnp.where(vals == m, lane, LANES), axis=-1, keepdims=True)
        return m, idx

    gmask = lane < MOE_GROUPS
    gmax, gidx = first_max(jnp.where(gmask, logits, neg))
    gden = jnp.sum(jnp.where(gmask, jnp.exp(logits - gmax), 0.0), axis=-1, keepdims=True)
    g_w = 1.0 / gden
    lo = MOE_GROUPS + MOE_EXPERTS_PER_GROUP * gidx
    emask = (lane >= lo) & (lane < lo + MOE_EXPERTS_PER_GROUP)
    elog = jnp.where(emask, logits, neg)
    m1, i1 = first_max(elog)
    m2, i2 = first_max(jnp.where(lane == i1, neg, elog))
    tt = jnp.exp(m2 - m1)
    w1 = g_w / (1.0 + tt)
    w2 = w1 * tt
    ri_ref[...] = jnp.where(lane == 0, i1 - MOE_GROUPS, jnp.where(lane == 1, i2 - MOE_GROUPS, 0))
    rw_ref[...] = jnp.where(lane == 0, w1, jnp.where(lane == 1, w2, 0.0))


def _proj_router(o, w_out, x, g_ffn, w_router, *, tm=512):
    t, d = x.shape
    tm = min(tm, t)
    assert t % tm == 0
    return pl.pallas_call(
        _proj_router_kernel,
        grid=(t // tm,),
        in_specs=[
            pl.BlockSpec((tm, d), lambda i: (i, 0)),
            pl.BlockSpec((d, d), lambda i: (0, 0)),
            pl.BlockSpec((tm, d), lambda i: (i, 0)),
            pl.BlockSpec((1, d), lambda i: (0, 0)),
            pl.BlockSpec((d, LANES), lambda i: (0, 0)),
        ],
        out_specs=[
            pl.BlockSpec((tm, d), lambda i: (i, 0)),
            pl.BlockSpec((tm, LANES), lambda i: (i, 0)),
            pl.BlockSpec((tm, LANES), lambda i: (i, 0)),
        ],
        out_shape=[jax.ShapeDtypeStruct((t, d), F32), jax.ShapeDtypeStruct((t, LANES), I32),
                   jax.ShapeDtypeStruct((t, LANES), F32)],
        compiler_params=_cparams(("parallel",)),
    )(o, w_out, x, g_ffn.reshape(1, d), w_router)


def _gather_rows(idx_ref, n, src_hbm, dst_ref, sem):
    def body(r, carry):
        t = idx_ref[0, 0, r]
        pltpu.make_async_copy(src_hbm.at[pl.ds(t, 1), :], dst_ref.at[pl.ds(r, 1), :], sem).start()
        return carry
    lax.fori_loop(0, n, body, 0, unroll=8)


def _wait_rows(n, src_hbm, dst_ref, sem):
    pltpu.make_async_copy(src_hbm.at[pl.ds(0, n), :], dst_ref, sem).wait()


def _moe_kernel(eid_ref, nused_ref, tok_ref, tok_next_ref, x_hbm, g_ref, wg_ref, wu_ref, wd_ref,
                o_ref, xbuf, sem, *, blk):
    b = pl.program_id(0)
    nb = pl.num_programs(0)
    slot = b % 2

    @pl.when(b == 0)
    def _():
        _gather_rows(tok_ref, blk, x_hbm, xbuf.at[0], sem.at[0])

    @pl.when(b + 1 < nb)
    def _():
        _gather_rows(tok_next_ref, blk, x_hbm, xbuf.at[1 - slot], sem.at[1 - slot])

    _wait_rows(blk, x_hbm, xbuf.at[slot], sem.at[slot])

    @pl.when(b < nused_ref[0])
    def _():
        x = xbuf[slot]
        ms = jnp.mean(x * x, axis=-1, keepdims=True)
        hn = ((x * lax.rsqrt(ms + RMS_EPS)) * g_ref[...]).astype(BF16)
        hg = jnp.dot(hn, wg_ref[0], preferred_element_type=F32)
        hu = jnp.dot(hn, wu_ref[0], preferred_element_type=F32)
        act = ((hg * jax.nn.sigmoid(hg)) * hu).astype(BF16)
        o_ref[...] = jnp.dot(act, wd_ref[0], preferred_element_type=F32)

    @pl.when(b >= nused_ref[0])
    def _():
        o_ref[...] = jnp.zeros_like(o_ref)


def _moe_ffn(x, g_ffn, blk_eid, n_used, row_tok, w_gate, w_up, w_down, *, blk):
    t, d = x.shape
    n_rows = row_tok.shape[0]
    n_blocks = n_rows // blk
    dff = w_gate.shape[-1]
    tok3 = row_tok.reshape(n_blocks, 1, blk)
    grid_spec = pltpu.PrefetchScalarGridSpec(
        num_scalar_prefetch=2,
        grid=(n_blocks,),
        in_specs=[
            pl.BlockSpec((1, 1, blk), lambda b, e, u: (b, 0, 0), memory_space=pltpu.SMEM),
            pl.BlockSpec((1, 1, blk), lambda b, e, u: (jnp.minimum(b + 1, n_blocks - 1), 0, 0),
                         memory_space=pltpu.SMEM),
            pl.BlockSpec(memory_space=pl.ANY),
            pl.BlockSpec((1, d), lambda b, e, u: (0, 0)),
            pl.BlockSpec((1, d, dff), lambda b, e, u: (e[b], 0, 0)),
            pl.BlockSpec((1, d, dff), lambda b, e, u: (e[b], 0, 0)),
            pl.BlockSpec((1, dff, d), lambda b, e, u: (e[b], 0, 0)),
        ],
        out_specs=pl.BlockSpec((blk, d), lambda b, e, u: (b, 0)),
        scratch_shapes=[pltpu.VMEM((2, blk, d), F32), pltpu.SemaphoreType.DMA((2,))],
    )
    return pl.pallas_call(
        functools.partial(_moe_kernel, blk=blk),
        grid_spec=grid_spec,
        out_shape=jax.ShapeDtypeStruct((n_rows, d), F32),
        compiler_params=_cparams(("arbitrary",)),
    )(blk_eid, n_used, tok3, tok3, x, g_ffn.reshape(1, d), w_gate, w_up, w_down)


def _combine_kernel(pos_ref, pos_next_ref, x_ref, rw_ref, y_hbm, g_ref, o_ref, ybuf, sem,
                    *, tc, final_norm):
    i = pl.program_id(0)
    n = pl.num_programs(0)
    slot = i % 2

    @pl.when(i == 0)
    def _():
        _gather_rows(pos_ref, 2 * tc, y_hbm, ybuf.at[0], sem.at[0])

    @pl.when(i + 1 < n)
    def _():
        _gather_rows(pos_next_ref, 2 * tc, y_hbm, ybuf.at[1 - slot], sem.at[1 - slot])

    _wait_rows(2 * tc, y_hbm, ybuf.at[slot], sem.at[slot])

    rw = rw_ref[...]
    y = x_ref[...] + rw[:, 0:1] * ybuf[slot, pl.ds(0, tc), :] + rw[:, 1:2] * ybuf[slot, pl.ds(tc, tc), :]
    if final_norm:
        ms = jnp.mean(y * y, axis=-1, keepdims=True)
        y = (y * lax.rsqrt(ms + RMS_EPS)) * g_ref[...]
    o_ref[...] = y


def _combine(x, route_w, pos, y_sorted, g_final, *, final_norm, tc=256):
    t, d = x.shape
    tc = min(tc, t)
    assert t % tc == 0
    nt = t // tc
    pos3 = pos.reshape(nt, tc, 2).transpose(0, 2, 1).reshape(nt, 1, 2 * tc)
    return pl.pallas_call(
        functools.partial(_combine_kernel, tc=tc, final_norm=final_norm),
        grid=(nt,),
        in_specs=[
            pl.BlockSpec((1, 1, 2 * tc), lambda i: (i, 0, 0), memory_space=pltpu.SMEM),
            pl.BlockSpec((1, 1, 2 * tc), lambda i: (jnp.minimum(i + 1, nt - 1), 0, 0),
                         memory_space=pltpu.SMEM),
            pl.BlockSpec((tc, d), lambda i: (i, 0)),
            pl.BlockSpec((tc, LANES), lambda i: (i, 0)),
            pl.BlockSpec(memory_space=pl.ANY),
            pl.BlockSpec((1, d), lambda i: (0, 0)),
        ],
        out_specs=pl.BlockSpec((tc, d), lambda i: (i, 0)),
        out_shape=jax.ShapeDtypeStruct((t, d), F32),
        scratch_shapes=[pltpu.VMEM((2, 2 * tc, d), F32), pltpu.SemaphoreType.DMA((2,))],
        compiler_params=_cparams(("arbitrary",)),
    )(pos3, pos3, x, route_w, y_sorted, g_final.reshape(1, d))


def _dispatch(route_i, blk):
    t = route_i.shape[0]
    eid = route_i[:, :2].reshape(-1)
    n_assign = eid.shape[0]
    onehot = (eid[:, None] == jnp.arange(MOE_N_EXPERTS, dtype=I32)[None, :]).astype(I32)
    csum = jnp.cumsum(onehot, axis=0)
    rank = jnp.sum(csum * onehot, axis=1) - 1
    counts = csum[-1]
    padded = ((counts + blk - 1) // blk) * blk
    pend = jnp.cumsum(padded)
    pstart = pend - padded
    dest = (pstart[eid] + rank).astype(I32)
    n_rows = n_assign + MOE_N_EXPERTS * blk
    n_blocks = n_rows // blk
    row_tok = jnp.zeros((n_rows,), I32).at[dest].set(jnp.arange(n_assign, dtype=I32) // 2)
    blk_start = jnp.arange(n_blocks, dtype=I32) * blk
    blk_eid = jnp.clip(jnp.searchsorted(pend, blk_start, side='right'), 0,
                       MOE_N_EXPERTS - 1).astype(I32)
    n_used = (pend[-1:] // blk).astype(I32)
    return row_tok, blk_eid, n_used, dest.reshape(t, 2)


def _router_weights(w_group, w_expert):
    d = w_group.shape[0]
    w = jnp.zeros((d, LANES), F32)
    w = w.at[:, :MOE_GROUPS].set(w_group).at[:, MOE_GROUPS:MOE_GROUPS + MOE_N_EXPERTS].set(w_expert)
    return w.astype(BF16)


def _moe_layer(x1, route_i, route_w, g_ffn, w_gate, w_up, w_down, g_final, *, final_norm, blk=256):
    row_tok, blk_eid, n_used, pos = _dispatch(route_i, blk)
    y_sorted = _moe_ffn(x1, g_ffn, blk_eid, n_used, row_tok, w_gate.astype(BF16),
                        w_up.astype(BF16), w_down.astype(BF16), blk=blk)
    return _combine(x1, route_w, pos, y_sorted, g_final, final_norm=final_norm)


def kernel(x, norm_mix, norm_ffn, gla_w_in, gla_w_gate_up, gla_b_gate, gla_norm, gla_w_out,
           diff_w_in, diff_lambda_q1, diff_lambda_k1, diff_lambda_q2, diff_lambda_k2,
           diff_subln, diff_w_out, moe_w_group, moe_w_expert, moe_w_gate, moe_w_up,
           moe_w_down, final_norm):
    batch, seq, d = x.shape
    t = batch * seq
    xf = x.reshape(t, d)

    dk_total = d // 2
    n_main = 2 * dk_total + 2 * d
    w_in = gla_w_in[0]
    gla_dk = dk_total // GLA_HEADS
    colscale = jnp.ones((n_main,), F32).at[:dk_total].set(gla_dk ** -0.5)
    rank = w_in.shape[1] - n_main
    w_gz = jnp.zeros((d, LANES), F32).at[:, :rank].set(w_in[:, n_main:]).astype(BF16)
    proj, gz = _norm_matmul(xf, norm_mix[0], w_in[:, :n_main].astype(BF16), colscale, w_gz)
    o = _gla_scan(proj, gz, gla_w_gate_up[0], gla_b_gate[0], gla_norm[0], batch=batch, seq=seq)
    x1, ri, rw = _proj_router(o, gla_w_out[0].astype(BF16), xf, norm_ffn[0],
                              _router_weights(moe_w_group[0], moe_w_expert[0]))
    x2 = _moe_layer(x1, ri, rw, norm_ffn[0], moe_w_gate[0], moe_w_up[0], moe_w_down[0],
                    final_norm, final_norm=False)

    hd = d // DIFF_HEADS // 2
    colscale = jnp.ones((3 * d,), F32).at[:d].set(hd ** -0.5 * LOG2E)
    qkv = _norm_matmul(x2, norm_mix[1], diff_w_in[0].astype(BF16), colscale)
    o = _diff_attention(qkv, diff_lambda_q1[0], diff_lambda_k1[0], diff_lambda_q2[0],
                        diff_lambda_k2[0], diff_subln[0], batch=batch, seq=seq, layer_idx=1)
    x3, ri, rw = _proj_router(o, diff_w_out[0].astype(BF16), x2, norm_ffn[1],
                              _router_weights(moe_w_group[1], moe_w_expert[1]))
    out = _moe_layer(x3, ri, rw, norm_ffn[1], moe_w_gate[1], moe_w_up[1], moe_w_down[1],
                     final_norm, final_norm=True)
    return out.reshape(batch, seq, d)
```

```python
import functools
import math

import jax
import jax.numpy as jnp
from jax import lax
from jax.experimental import pallas as pl
from jax.experimental.pallas import tpu as pltpu

F32 = jnp.float32
BF16 = jnp.bfloat16
I32 = jnp.int32

RMS_EPS = 1e-6
CHUNK = 64
GLA_HEADS = 4
GLA_GATE_TAU = 16.0
DIFF_HEADS = 8
MOE_GROUPS = 4
MOE_EXPERTS_PER_GROUP = 8
MOE_N_EXPERTS = MOE_GROUPS * MOE_EXPERTS_PER_GROUP

LANES = 128
SLAB = 256
VMEM_LIMIT = 56 * 1024 * 1024
LOG2E = 1.4426950408889634
NEG_BIG = -1e30
EXP_ZERO = 160.0
NORM_SLACK = 1.0 + 2.0 ** -10
SQ_SLACK = 1.0 + 2.0 ** -7

HIGHEST = lax.Precision.HIGHEST


def _cparams(sem):
    return pltpu.CompilerParams(dimension_semantics=sem, vmem_limit_bytes=VMEM_LIMIT)


def _norm_matmul_kernel(x_ref, g_ref, w_ref, cs_ref, *rest, n_slab, has_aux):
    if has_aux:
        w2_ref, o_ref, o2_ref, hn_ref = rest
    else:
        o_ref, hn_ref = rest

    @pl.when(pl.program_id(1) == 0)
    def _():
        x = x_ref[...]
        ms = jnp.mean(x * x, axis=-1, keepdims=True)
        hn = ((x * lax.rsqrt(ms + RMS_EPS)) * g_ref[...]).astype(BF16)
        hn_ref[...] = hn
        if has_aux:
            o2_ref[...] = jnp.dot(hn, w2_ref[...], preferred_element_type=F32)

    acc = jnp.dot(hn_ref[...], w_ref[...], preferred_element_type=F32) * cs_ref[...]
    for s in range(n_slab):
        o_ref[s] = acc[:, s * SLAB:(s + 1) * SLAB].astype(BF16)


def _norm_matmul(x, g, w, colscale, w_aux=None, *, tm=1024, tn=1024):
    t, d = x.shape
    n = w.shape[1]
    tm = min(tm, t)
    assert t % tm == 0 and n % tn == 0 and tn % SLAB == 0
    n_slab = tn // SLAB
    has_aux = w_aux is not None
    in_specs = [
        pl.BlockSpec((tm, d), lambda i, j: (i, 0)),
        pl.BlockSpec((1, d), lambda i, j: (0, 0)),
        pl.BlockSpec((d, tn), lambda i, j: (0, j)),
        pl.BlockSpec((1, tn), lambda i, j: (0, j)),
    ]
    out_shape = [jax.ShapeDtypeStruct((n // SLAB, t, SLAB), BF16)]
    out_specs = [pl.BlockSpec((n_slab, tm, SLAB), lambda i, j: (j, i, 0))]
    args = [x, g.reshape(1, d), w, colscale.reshape(1, n)]
    if has_aux:
        in_specs.append(pl.BlockSpec((d, LANES), lambda i, j: (0, 0)))
        out_shape.append(jax.ShapeDtypeStruct((t, LANES), F32))
        out_specs.append(pl.BlockSpec((tm, LANES), lambda i, j: (i, 0)))
        args.append(w_aux)
    res = pl.pallas_call(
        functools.partial(_norm_matmul_kernel, n_slab=n_slab, has_aux=has_aux),
        grid=(t // tm, n // tn),
        in_specs=in_specs,
        out_specs=out_specs,
        out_shape=out_shape,
        scratch_shapes=[pltpu.VMEM((tm, d), BF16)],
        compiler_params=_cparams(("parallel", "arbitrary")),
    )(*args)
    return res if has_aux else res[0]


def _gla_kernel(q_ref, k_ref, v_ref, r_ref, gz_ref, wg_ref, bg_ref, ng_ref, o_ref,
                state_ref, la_ref, ob_ref, *, n_chunks):
    @pl.when(pl.program_id(2) == 0)
    def _():
        state_ref[...] = jnp.zeros_like(state_ref)

    z = jnp.dot(gz_ref[...], wg_ref[...], preferred_element_type=F32,
                precision=HIGHEST) + bg_ref[...]
    la_ref[...] = -(jnp.maximum(-z, 0.0) + jnp.log1p(jnp.exp(-jnp.abs(z)))) * (1.0 / GLA_GATE_TAU)

    row = lax.broadcasted_iota(I32, (CHUNK, CHUNK), 0)
    col = lax.broadcasted_iota(I32, (CHUNK, CHUNK), 1)
    later = (col > row).astype(F32)
    ones = jnp.ones((CHUNK, LANES), F32)
    dv = state_ref.shape[1]

    def body(i, carry):
        sl = pl.ds(pl.multiple_of(i * CHUNK, CHUNK), CHUNK)
        la = la_ref[sl, :]
        suf = jnp.dot(later, la, preferred_element_type=F32, precision=HIGHEST)
        tot = lax.dot_general(la, ones, (((0,), (0,)), ((), ())),
                              preferred_element_type=F32, precision=HIGHEST)
        k_dec = (k_ref[0, sl, :].astype(F32) * jnp.exp(suf)).astype(BF16)
        v = jnp.concatenate([v_ref[0, sl, :], v_ref[1, sl, :]], axis=-1)
        upd = lax.dot_general(k_dec, v, (((0,), (0,)), ((), ())),
                              preferred_element_type=F32)
        decay = jnp.exp(tot)
        decay = jnp.concatenate([decay] * (dv // LANES), axis=-1)
        st = decay * state_ref[...] + upd
        state_ref[...] = st
        ob_ref[sl, :] = jnp.dot(q_ref[0, sl, :], st.astype(BF16), preferred_element_type=F32)
        return carry

    lax.fori_loop(0, n_chunks, body, 0)

    o = ob_ref[...]
    ms = jnp.mean(o * o, axis=-1, keepdims=True)
    on = (o * lax.rsqrt(ms + RMS_EPS)) * ng_ref[...]
    r = jnp.concatenate([r_ref[0], r_ref[1]], axis=-1).astype(F32)
    o_ref[...] = (on * (r * jax.nn.sigmoid(r))).astype(BF16)


def _gla_scan(proj, gz, w_gate_up, b_gate, norm_g, *, batch, seq, lc=512):
    t = batch * seq
    lc = min(lc, seq)
    assert seq % lc == 0 and lc % CHUNK == 0
    ncb = seq // lc
    dk = SLAB
    dv = 2 * SLAB
    d_out = GLA_HEADS * dv
    rank = w_gate_up.shape[0]
    wg = jnp.zeros((LANES, GLA_HEADS * dk), F32).at[:rank].set(w_gate_up)

    def tok(b, h, c):
        return b * ncb + c

    return pl.pallas_call(
        functools.partial(_gla_kernel, n_chunks=lc // CHUNK),
        grid=(batch, GLA_HEADS, ncb),
        in_specs=[
            pl.BlockSpec((1, lc, SLAB), lambda b, h, c: (h, tok(b, h, c), 0)),
            pl.BlockSpec((1, lc, SLAB), lambda b, h, c: (GLA_HEADS + h, tok(b, h, c), 0)),
            pl.BlockSpec((2, lc, SLAB), lambda b, h, c: (GLA_HEADS + h, tok(b, h, c), 0)),
            pl.BlockSpec((2, lc, SLAB), lambda b, h, c: (2 * GLA_HEADS + h, tok(b, h, c), 0)),
            pl.BlockSpec((lc, LANES), lambda b, h, c: (tok(b, h, c), 0)),
            pl.BlockSpec((LANES, dk), lambda b, h, c: (0, h)),
            pl.BlockSpec((1, dk), lambda b, h, c: (0, h)),
            pl.BlockSpec((1, dv), lambda b, h, c: (0, 0)),
        ],
        out_specs=pl.BlockSpec((lc, dv), lambda b, h, c: (tok(b, h, c), h)),
        out_shape=jax.ShapeDtypeStruct((t, d_out), BF16),
        scratch_shapes=[pltpu.VMEM((dk, dv), F32), pltpu.VMEM((lc, dk), F32),
                        pltpu.VMEM((lc, dv), F32)],
        compiler_params=_cparams(("parallel", "parallel", "arbitrary")),
    )(proj, proj, proj, proj, gz, wg, b_gate.reshape(1, -1), norm_g.reshape(1, -1))


def _diff_attn_kernel(slope_ref, invct_ref, q_ref, k_ref, v_ref, qfeat_ref, kfeat_ref, lq1_ref, lk1_ref,
                      lq2_ref, lk2_ref, sg_ref, o_ref, vt_ref, dbias_ref, r0_ref, kmax_ref, m_ref, l_ref, acc_ref,
                      *, tq, lambda_init):
    h = pl.program_id(1)
    qi = pl.program_id(2)
    hd = SLAB // 2
    c = slope_ref[h] * LOG2E
    nt = (((1,), (1,)), ((), ()))

    ones16 = jnp.ones((16, hd), BF16)

    def sq_norms(x):
        xf = x.astype(F32)
        sq = (xf * xf * SQ_SLACK).astype(BF16)
        return lax.dot_general(ones16, sq, nt, preferred_element_type=F32)[0:1]

    @pl.when(qi == 0)
    def _():
        def per_block(i, carry):
            sl = pl.ds(pl.multiple_of(i * tq, tq), tq)
            vt_ref[:, sl] = v_ref[0, sl, :].T
            kb = k_ref[0, sl, :]
            return tuple(jnp.maximum(carry[u], sq_norms(kb[:, u * hd:(u + 1) * hd]))
                         for u in range(2))
        zero = jnp.zeros((1, tq), F32)
        kn2 = lax.fori_loop(0, v_ref.shape[1] // tq, per_block, (zero, zero))
        for u in range(2):
            kmax_ref[u] = jnp.broadcast_to(jnp.sqrt(jnp.max(kn2[u], axis=1, keepdims=True)),
                                           (8, LANES))
        ki = lax.broadcasted_iota(I32, (tq, tq), 0)
        qj = lax.broadcasted_iota(I32, (tq, tq), 1)
        bias = c * (qj - jnp.abs(qj - ki)).astype(F32)
        dbias_ref[...] = jnp.where((qj // CHUNK) >= (ki // CHUNK), bias, -jnp.inf)

    q = q_ref[0]
    start = pl.multiple_of(qi * tq, tq)

    k = k_ref[0, pl.ds(start, tq), :]
    vt = vt_ref[:, pl.ds(start, tq)]
    for u in range(2):
        r = lax.dot_general(k[:, u * hd:(u + 1) * hd], q[:, u * hd:(u + 1) * hd], nt,
                            preferred_element_type=F32) + dbias_ref[...]
        m_new = jnp.max(r, axis=0, keepdims=True)
        p = jnp.exp2(r - m_new)
        l_ref[u] = jnp.sum(p, axis=0, keepdims=True)
        acc_ref[u] = jnp.dot(vt, p.astype(BF16), preferred_element_type=F32)
        m_ref[u] = m_new

    qfeat = jnp.broadcast_to(qfeat_ref[0], (tq, LANES))
    kfeat = kfeat_ref[...]
    qa = [jnp.concatenate([q[:, u * hd:(u + 1) * hd], qfeat], axis=-1) for u in range(2)]

    def scores(u, blk):
        st = pl.multiple_of(blk * tq, tq)
        ka = jnp.concatenate([k_ref[0, pl.ds(st, tq), u * hd:(u + 1) * hd], kfeat], axis=-1)
        return lax.dot_general(ka, qa[u], nt, preferred_element_type=F32)

    def accumulate(u, r, vt, off):
        m_prev = m_ref[u]
        m_new = jnp.maximum(m_prev, jnp.max(r, axis=0, keepdims=True) - off)
        alpha = jnp.exp2(m_prev - m_new)
        p = jnp.exp2(r - (m_new + off))
        l_ref[u] = alpha * l_ref[u] + jnp.sum(p, axis=0, keepdims=True)
        acc_ref[u] = alpha * acc_ref[u] + jnp.dot(vt, p.astype(BF16),
                                                  preferred_element_type=F32)
        m_ref[u] = m_new

    r0_ref[...] = scores(0, jnp.maximum(qi - 1, 0))

    def past_block(dd, carry):
        blk = qi - dd
        vt = vt_ref[:, pl.ds(pl.multiple_of(blk * tq, tq), tq)]
        off = c * (dd * tq).astype(F32)
        r1 = scores(1, blk)
        accumulate(0, r0_ref[...], vt, off)
        r0_ref[...] = scores(0, jnp.maximum(blk - 1, 0))
        accumulate(1, r1, vt, off)
        return carry

    gap = None
    for u in range(2):
        bound = (jnp.sqrt(sq_norms(q[:, u * hd:(u + 1) * hd])) * kmax_ref[u][0:1, 0:1]
                 * NORM_SLACK - m_ref[u])
        g = jnp.max(bound, axis=1, keepdims=True)
        gap = g if gap is None else jnp.maximum(gap, g)
    n_need = jnp.ceil((gap + EXP_ZERO) * invct_ref[h])
    n_need = jnp.clip(n_need, 0.0, float(v_ref.shape[1] // tq)).astype(I32)[0, 0]
    lax.fori_loop(1, jnp.minimum(qi, n_need) + 1, past_block, 0)

    lam = (jnp.exp(jnp.sum(lq1_ref[...] * lk1_ref[...], axis=-1, keepdims=True))
           - jnp.exp(jnp.sum(lq2_ref[...] * lk2_ref[...], axis=-1, keepdims=True))
           + lambda_init)
    o = acc_ref[0] * (1.0 / l_ref[0]) - lam * (acc_ref[1] * (1.0 / l_ref[1]))
    o = o.T
    ms = jnp.mean(o * o, axis=-1, keepdims=True)
    o = (o * lax.rsqrt(ms + RMS_EPS)) * sg_ref[...]
    o_ref[...] = (o * (1.0 - lambda_init)).astype(BF16)


def _alibi_features(slopes, tq):
    c = slopes * LOG2E
    c_hi = c.astype(BF16).astype(F32)
    c_lo = c - c_hi
    qf = jnp.zeros((slopes.shape[0], 1, LANES), F32)
    qf = qf.at[:, 0, 0].set(LANES * c_hi).at[:, 0, 1].set(c_hi)
    qf = qf.at[:, 0, 2].set(LANES * c_lo).at[:, 0, 3].set(c_lo)
    idx = jnp.arange(tq, dtype=I32)
    hi = (idx // LANES).astype(F32)
    lo = (idx % LANES).astype(F32)
    kf = jnp.zeros((tq, LANES), F32).at[:, 0].set(hi).at[:, 1].set(lo).at[:, 2].set(hi).at[:, 3].set(lo)
    return qf.astype(BF16), kf.astype(BF16)


def _diff_attention(qkv, lq1, lk1, lq2, lk2, subln, *, batch, seq, layer_idx, tq=512):
    t = batch * seq
    tq = min(tq, seq)
    assert seq % tq == 0 and tq % CHUNK == 0
    nq = seq // tq
    lambda_init = 0.8 - 0.6 * math.exp(-0.3 * layer_idx)
    slopes = 2.0 ** (-8.0 * jnp.arange(1, DIFF_HEADS + 1, dtype=F32) / DIFF_HEADS)
    vec = lambda a: a.reshape(1, -1).astype(F32)
    qfeat, kfeat = _alibi_features(slopes, tq)
    grid_spec = pltpu.PrefetchScalarGridSpec(
        num_scalar_prefetch=2,
        grid=(batch, DIFF_HEADS, nq),
        in_specs=[
            pl.BlockSpec((1, tq, SLAB), lambda b, h, i, s, n: (h, b * nq + i, 0)),
            pl.BlockSpec((1, seq, SLAB), lambda b, h, i, s, n: (DIFF_HEADS + h, b, 0)),
            pl.BlockSpec((1, seq, SLAB), lambda b, h, i, s, n: (2 * DIFF_HEADS + h, b, 0)),
            pl.BlockSpec((1, 1, LANES), lambda b, h, i, s, n: (h, 0, 0)),
            pl.BlockSpec((tq, LANES), lambda b, h, i, s, n: (0, 0)),
            pl.BlockSpec((1, SLAB // 2), lambda b, h, i, s, n: (0, 0)),
            pl.BlockSpec((1, SLAB // 2), lambda b, h, i, s, n: (0, 0)),
            pl.BlockSpec((1, SLAB // 2), lambda b, h, i, s, n: (0, 0)),
            pl.BlockSpec((1, SLAB // 2), lambda b, h, i, s, n: (0, 0)),
            pl.BlockSpec((1, SLAB), lambda b, h, i, s, n: (0, 0)),
        ],
        out_specs=pl.BlockSpec((tq, SLAB), lambda b, h, i, s, n: (b * nq + i, h)),
        scratch_shapes=[pltpu.VMEM((SLAB, seq), BF16), pltpu.VMEM((tq, tq), F32),
                        pltpu.VMEM((tq, tq), F32), pltpu.VMEM((2, 8, LANES), F32),
                        pltpu.VMEM((2, 1, tq), F32), pltpu.VMEM((2, 1, tq), F32),
                        pltpu.VMEM((2, SLAB, tq), F32)],
    )
    return pl.pallas_call(
        functools.partial(_diff_attn_kernel, tq=tq, lambda_init=lambda_init),
        grid_spec=grid_spec,
        out_shape=jax.ShapeDtypeStruct((t, DIFF_HEADS * SLAB), BF16),
        compiler_params=_cparams(("parallel", "parallel", "arbitrary")),
    )(slopes, 1.0 / (slopes * (LOG2E * tq)), qkv, qkv, qkv, qfeat, kfeat, vec(lq1), vec(lk1), vec(lq2), vec(lk2), vec(subln))


def _proj_router_kernel(o_ref, w_ref, x_ref, g_ref, wr_ref, x1_ref, ri_ref, rw_ref):
    x1 = x_ref[...] + jnp.dot(o_ref[...], w_ref[...], preferred_element_type=F32)
    x1_ref[...] = x1
    ms = jnp.mean(x1 * x1, axis=-1, keepdims=True)
    hn = ((x1 * lax.rsqrt(ms + RMS_EPS)) * g_ref[...]).astype(BF16)
    logits = jnp.dot(hn, wr_ref[...], preferred_element_type=F32)
    lane = lax.broadcasted_iota(I32, logits.shape, 1)
    neg = -jnp.inf

    def first_max(vals):
        m = jnp.max(vals, axis=-1, keepdims=True)
        idx = jnp.min(jnp.where(vals == m, lane, LANES), axis=-1, keepdims=True)
        return m, idx

    gmask = lane < MOE_GROUPS
    gmax, gidx = first_max(jnp.where(gmask, logits, neg))
    gden = jnp.sum(jnp.where(gmask, jnp.exp(logits - gmax), 0.0), axis=-1, keepdims=True)
    g_w = 1.0 / gden
    lo = MOE_GROUPS + MOE_EXPERTS_PER_GROUP * gidx
    emask = (lane >= lo) & (lane < lo + MOE_EXPERTS_PER_GROUP)
    elog = jnp.where(emask, logits, neg)
    m1, i1 = first_max(elog)
    m2, i2 = first_max(jnp.where(lane == i1, neg, elog))
    tt = jnp.exp(m2 - m1)
    w1 = g_w / (1.0 + tt)
    w2 = w1 * tt
    ri_ref[...] = jnp.where(lane == 0, i1 - MOE_GROUPS, jnp.where(lane == 1, i2 - MOE_GROUPS, 0))
    rw_ref[...] = jnp.where(lane == 0, w1, jnp.where(lane == 1, w2, 0.0))


def _proj_router(o, w_out, x, g_ffn, w_router, *, tm=512):
    t, d = x.shape
    tm = min(tm, t)
    assert t % tm == 0
    return pl.pallas_call(
        _proj_router_kernel,
        grid=(t // tm,),
        in_specs=[
            pl.BlockSpec((tm, d), lambda i: (i, 0)),
            pl.BlockSpec((d, d), lambda i: (0, 0)),
            pl.BlockSpec((tm, d), lambda i: (i, 0)),
            pl.BlockSpec((1, d), lambda i: (0, 0)),
            pl.BlockSpec((d, LANES), lambda i: (0, 0)),
        ],
        out_specs=[
            pl.BlockSpec((tm, d), lambda i: (i, 0)),
            pl.BlockSpec((tm, LANES), lambda i: (i, 0)),
            pl.BlockSpec((tm, LANES), lambda i: (i, 0)),
        ],
        out_shape=[jax.ShapeDtypeStruct((t, d), F32), jax.ShapeDtypeStruct((t, LANES), I32),
                   jax.ShapeDtypeStruct((t, LANES), F32)],
        compiler_params=_cparams(("parallel",)),
    )(o, w_out, x, g_ffn.reshape(1, d), w_router)


def _gather_rows(idx_ref, n, src_hbm, dst_ref, sem):
    def body(r, carry):
        t = idx_ref[0, 0, r]
        pltpu.make_async_copy(src_hbm.at[pl.ds(t, 1), :], dst_ref.at[pl.ds(r, 1), :], sem).start()
        return carry
    lax.fori_loop(0, n, body, 0, unroll=8)


def _wait_rows(n, src_hbm, dst_ref, sem):
    pltpu.make_async_copy(src_hbm.at[pl.ds(0, n), :], dst_ref, sem).wait()


def _moe_kernel(eid_ref, nused_ref, tok_ref, tok_next_ref, x_hbm, g_ref, wg_ref, wu_ref, wd_ref,
                o_ref, xbuf, sem, *, blk):
    b = pl.program_id(0)
    nb = pl.num_programs(0)
    slot = b % 2

    @pl.when(b == 0)
    def _():
        _gather_rows(tok_ref, blk, x_hbm, xbuf.at[0], sem.at[0])

    @pl.when(b + 1 < nb)
    def _():
        _gather_rows(tok_next_ref, blk, x_hbm, xbuf.at[1 - slot], sem.at[1 - slot])

    _wait_rows(blk, x_hbm, xbuf.at[slot], sem.at[slot])

    @pl.when(b < nused_ref[0])
    def _():
        x = xbuf[slot]
        ms = jnp.mean(x * x, axis=-1, keepdims=True)
        hn = ((x * lax.rsqrt(ms + RMS_EPS)) * g_ref[...]).astype(BF16)
        hg = jnp.dot(hn, wg_ref[0], preferred_element_type=F32)
        hu = jnp.dot(hn, wu_ref[0], preferred_element_type=F32)
        act = ((hg * jax.nn.sigmoid(hg)) * hu).astype(BF16)
        o_ref[...] = jnp.dot(act, wd_ref[0], preferred_element_type=F32)

    @pl.when(b >= nused_ref[0])
    def _():
        o_ref[...] = jnp.zeros_like(o_ref)


def _moe_ffn(x, g_ffn, blk_eid, n_used, row_tok, w_gate, w_up, w_down, *, blk):
    t, d = x.shape
    n_rows = row_tok.shape[0]
    n_blocks = n_rows // blk
    dff = w_gate.shape[-1]
    tok3 = row_tok.reshape(n_blocks, 1, blk)
    grid_spec = pltpu.PrefetchScalarGridSpec(
        num_scalar_prefetch=2,
        grid=(n_blocks,),
        in_specs=[
            pl.BlockSpec((1, 1, blk), lambda b, e, u: (b, 0, 0), memory_space=pltpu.SMEM),
            pl.BlockSpec((1, 1, blk), lambda b, e, u: (jnp.minimum(b + 1, n_blocks - 1), 0, 0),
                         memory_space=pltpu.SMEM),
            pl.BlockSpec(memory_space=pl.ANY),
            pl.BlockSpec((1, d), lambda b, e, u: (0, 0)),
            pl.BlockSpec((1, d, dff), lambda b, e, u: (e[b], 0, 0)),
            pl.BlockSpec((1, d, dff), lambda b, e, u: (e[b], 0, 0)),
            pl.BlockSpec((1, dff, d), lambda b, e, u: (e[b], 0, 0)),
        ],
        out_specs=pl.BlockSpec((blk, d), lambda b, e, u: (b, 0)),
        scratch_shapes=[pltpu.VMEM((2, blk, d), F32), pltpu.SemaphoreType.DMA((2,))],
    )
    return pl.pallas_call(
        functools.partial(_moe_kernel, blk=blk),
        grid_spec=grid_spec,
        out_shape=jax.ShapeDtypeStruct((n_rows, d), F32),
        compiler_params=_cparams(("arbitrary",)),
    )(blk_eid, n_used, tok3, tok3, x, g_ffn.reshape(1, d), w_gate, w_up, w_down)


def _combine_kernel(pos_ref, pos_next_ref, x_ref, rw_ref, y_hbm, g_ref, o_ref, ybuf, sem,
                    *, tc, final_norm):
    i = pl.program_id(0)
    n = pl.num_programs(0)
    slot = i % 2

    @pl.when(i == 0)
    def _():
        _gather_rows(pos_ref, 2 * tc, y_hbm, ybuf.at[0], sem.at[0])

    @pl.when(i + 1 < n)
    def _():
        _gather_rows(pos_next_ref, 2 * tc, y_hbm, ybuf.at[1 - slot], sem.at[1 - slot])

    _wait_rows(2 * tc, y_hbm, ybuf.at[slot], sem.at[slot])

    rw = rw_ref[...]
    y = x_ref[...] + rw[:, 0:1] * ybuf[slot, pl.ds(0, tc), :] + rw[:, 1:2] * ybuf[slot, pl.ds(tc, tc), :]
    if final_norm:
        ms = jnp.mean(y * y, axis=-1, keepdims=True)
        y = (y * lax.rsqrt(ms + RMS_EPS)) * g_ref[...]
    o_ref[...] = y


def _combine(x, route_w, pos, y_sorted, g_final, *, final_norm, tc=256):
    t, d = x.shape
    tc = min(tc, t)
    assert t % tc == 0
    nt = t // tc
    pos3 = pos.reshape(nt, tc, 2).transpose(0, 2, 1).reshape(nt, 1, 2 * tc)
    return pl.pallas_call(
        functools.partial(_combine_kernel, tc=tc, final_norm=final_norm),
        grid=(nt,),
        in_specs=[
            pl.BlockSpec((1, 1, 2 * tc), lambda i: (i, 0, 0), memory_space=pltpu.SMEM),
            pl.BlockSpec((1, 1, 2 * tc), lambda i: (jnp.minimum(i + 1, nt - 1), 0, 0),
                         memory_space=pltpu.SMEM),
            pl.BlockSpec((tc, d), lambda i: (i, 0)),
            pl.BlockSpec((tc, LANES), lambda i: (i, 0)),
            pl.BlockSpec(memory_space=pl.ANY),
            pl.BlockSpec((1, d), lambda i: (0, 0)),
        ],
        out_specs=pl.BlockSpec((tc, d), lambda i: (i, 0)),
        out_shape=jax.ShapeDtypeStruct((t, d), F32),
        scratch_shapes=[pltpu.VMEM((2, 2 * tc, d), F32), pltpu.SemaphoreType.DMA((2,))],
        compiler_params=_cparams(("arbitrary",)),
    )(pos3, pos3, x, route_w, y_sorted, g_final.reshape(1, d))


def _dispatch(route_i, blk):
    t = route_i.shape[0]
    eid = route_i[:, :2].reshape(-1)
    n_assign = eid.shape[0]
    onehot = (eid[:, None] == jnp.arange(MOE_N_EXPERTS, dtype=I32)[None, :]).astype(I32)
    csum = jnp.cumsum(onehot, axis=0)
    rank = jnp.sum(csum * onehot, axis=1) - 1
    counts = csum[-1]
    padded = ((counts + blk - 1) // blk) * blk
    pend = jnp.cumsum(padded)
    pstart = pend - padded
    dest = (pstart[eid] + rank).astype(I32)
    n_rows = n_assign + MOE_N_EXPERTS * blk
    n_blocks = n_rows // blk
    row_tok = jnp.zeros((n_rows,), I32).at[dest].set(jnp.arange(n_assign, dtype=I32) // 2)
    blk_start = jnp.arange(n_blocks, dtype=I32) * blk
    blk_eid = jnp.clip(jnp.searchsorted(pend, blk_start, side='right'), 0,
                       MOE_N_EXPERTS - 1).astype(I32)
    n_used = (pend[-1:] // blk).astype(I32)
    return row_tok, blk_eid, n_used, dest.reshape(t, 2)


def _router_weights(w_group, w_expert):
    d = w_group.shape[0]
    w = jnp.zeros((d, LANES), F32)
    w = w.at[:, :MOE_GROUPS].set(w_group).at[:, MOE_GROUPS:MOE_GROUPS + MOE_N_EXPERTS].set(w_expert)
    return w.astype(BF16)


def _moe_layer(x1, route_i, route_w, g_ffn, w_gate, w_up, w_down, g_final, *, final_norm, blk=256):
    row_tok, blk_eid, n_used, pos = _dispatch(route_i, blk)
    y_sorted = _moe_ffn(x1, g_ffn, blk_eid, n_used, row_tok, w_gate.astype(BF16),
                        w_up.astype(BF16), w_down.astype(BF16), blk=blk)
    return _combine(x1, route_w, pos, y_sorted, g_final, final_norm=final_norm)


def kernel(x, norm_mix, norm_ffn, gla_w_in, gla_w_gate_up, gla_b_gate, gla_norm, gla_w_out,
           diff_w_in, diff_lambda_q1, diff_lambda_k1, diff_lambda_q2, diff_lambda_k2,
           diff_subln, diff_w_out, moe_w_group, moe_w_expert, moe_w_gate, moe_w_up,
           moe_w_down, final_norm):
    batch, seq, d = x.shape
    t = batch * seq
    xf = x.reshape(t, d)

    dk_total = d // 2
    n_main = 2 * dk_total + 2 * d
    w_in = gla_w_in[0]
    gla_dk = dk_total // GLA_HEADS
    colscale = jnp.ones((n_main,), F32).at[:dk_total].set(gla_dk ** -0.5)
    rank = w_in.shape[1] - n_main
    w_gz = jnp.zeros((d, LANES), F32).at[:, :rank].set(w_in[:, n_main:]).astype(BF16)
    proj, gz = _norm_matmul(xf, norm_mix[0], w_in[:, :n_main].astype(BF16), colscale, w_gz)
    o = _gla_scan(proj, gz, gla_w_gate_up[0], gla_b_gate[0], gla_norm[0], batch=batch, seq=seq)
    x1, ri, rw = _proj_router(o, gla_w_out[0].astype(BF16), xf, norm_ffn[0],
                              _router_weights(moe_w_group[0], moe_w_expert[0]))
    x2 = _moe_layer(x1, ri, rw, norm_ffn[0], moe_w_gate[0], moe_w_up[0], moe_w_down[0],
                    final_norm, final_norm=False)

    hd = d // DIFF_HEADS // 2
    colscale = jnp.ones((3 * d,), F32).at[:d].set(hd ** -0.5 * LOG2E)
    qkv = _norm_matmul(x2, norm_mix[1], diff_w_in[0].astype(BF16), colscale)
    o = _diff_attention(qkv, diff_lambda_q1[0], diff_lambda_k1[0], diff_lambda_q2[0],
                        diff_lambda_k2[0], diff_subln[0], batch=batch, seq=seq, layer_idx=1)
    x3, ri, rw = _proj_router(o, diff_w_out[0].astype(BF16), x2, norm_ffn[1],
                              _router_weights(moe_w_group[1], moe_w_expert[1]))
    out = _moe_layer(x3, ri, rw, norm_ffn[1], moe_w_gate[1], moe_w_up[1], moe_w_down[1],
                     final_norm, final_norm=True)
    return out.reshape(batch, seq, d)
```

```python
import functools
import math

import jax
import jax.numpy as jnp
from jax import lax
from jax.experimental import pallas as pl
from jax.experimental.pallas import tpu as pltpu

F32 = jnp.float32
BF16 = jnp.bfloat16
I32 = jnp.int32

RMS_EPS = 1e-6
CHUNK = 64
GLA_HEADS = 4
GLA_GATE_TAU = 16.0
DIFF_HEADS = 8
MOE_GROUPS = 4
MOE_EXPERTS_PER_GROUP = 8
MOE_N_EXPERTS = MOE_GROUPS * MOE_EXPERTS_PER_GROUP

LANES = 128
SLAB = 256
VMEM_LIMIT = 56 * 1024 * 1024
LOG2E = 1.4426950408889634
NEG_BIG = -1e30
EXP_ZERO = 160.0
NORM_SLACK = 1.0 + 2.0 ** -10
SQ_SLACK = 1.0 + 2.0 ** -7

HIGHEST = lax.Precision.HIGHEST


def _cparams(sem):
    return pltpu.CompilerParams(dimension_semantics=sem, vmem_limit_bytes=VMEM_LIMIT)


def _norm_matmul_kernel(x_ref, g_ref, w_ref, cs_ref, *rest, n_slab, has_aux):
    if has_aux:
        w2_ref, o_ref, o2_ref, hn_ref = rest
    else:
        o_ref, hn_ref = rest

    @pl.when(pl.program_id(1) == 0)
    def _():
        x = x_ref[...]
        ms = jnp.mean(x * x, axis=-1, keepdims=True)
        hn = ((x * lax.rsqrt(ms + RMS_EPS)) * g_ref[...]).astype(BF16)
        hn_ref[...] = hn
        if has_aux:
            o2_ref[...] = jnp.dot(hn, w2_ref[...], preferred_element_type=F32)

    acc = jnp.dot(hn_ref[...], w_ref[...], preferred_element_type=F32) * cs_ref[...]
    for s in range(n_slab):
        o_ref[s] = acc[:, s * SLAB:(s + 1) * SLAB].astype(BF16)


def _norm_matmul(x, g, w, colscale, w_aux=None, *, tm=1024, tn=1024):
    t, d = x.shape
    n = w.shape[1]
    tm = min(tm, t)
    assert t % tm == 0 and n % tn == 0 and tn % SLAB == 0
    n_slab = tn // SLAB
    has_aux = w_aux is not None
    in_specs = [
        pl.BlockSpec((tm, d), lambda i, j: (i, 0)),
        pl.BlockSpec((1, d), lambda i, j: (0, 0)),
        pl.BlockSpec((d, tn), lambda i, j: (0, j)),
        pl.BlockSpec((1, tn), lambda i, j: (0, j)),
    ]
    out_shape = [jax.ShapeDtypeStruct((n // SLAB, t, SLAB), BF16)]
    out_specs = [pl.BlockSpec((n_slab, tm, SLAB), lambda i, j: (j, i, 0))]
    args = [x, g.reshape(1, d), w, colscale.reshape(1, n)]
    if has_aux:
        in_specs.append(pl.BlockSpec((d, LANES), lambda i, j: (0, 0)))
        out_shape.append(jax.ShapeDtypeStruct((t, LANES), F32))
        out_specs.append(pl.BlockSpec((tm, LANES), lambda i, j: (i, 0)))
        args.append(w_aux)
    res = pl.pallas_call(
        functools.partial(_norm_matmul_kernel, n_slab=n_slab, has_aux=has_aux),
        grid=(t // tm, n // tn),
        in_specs=in_specs,
        out_specs=out_specs,
        out_shape=out_shape,
        scratch_shapes=[pltpu.VMEM((tm, d), BF16)],
        compiler_params=_cparams(("parallel", "arbitrary")),
    )(*args)
    return res if has_aux else res[0]


def _gla_kernel(q_ref, k_ref, v_ref, r_ref, gz_ref, wg_ref, bg_ref, ng_ref, o_ref,
                state_ref, la_ref, ob_ref, *, n_chunks):
    @pl.when(pl.program_id(2) == 0)
    def _():
        state_ref[...] = jnp.zeros_like(state_ref)

    z = jnp.dot(gz_ref[...], wg_ref[...], preferred_element_type=F32,
                precision=HIGHEST) + bg_ref[...]
    la_ref[...] = -(jnp.maximum(-z, 0.0) + jnp.log1p(jnp.exp(-jnp.abs(z)))) * (1.0 / GLA_GATE_TAU)

    row = lax.broadcasted_iota(I32, (CHUNK, CHUNK), 0)
    col = lax.broadcasted_iota(I32, (CHUNK, CHUNK), 1)
    later = (col > row).astype(F32)
    ones = jnp.ones((CHUNK, LANES), F32)
    dv = state_ref.shape[1]

    def body(i, carry):
        sl = pl.ds(pl.multiple_of(i * CHUNK, CHUNK), CHUNK)
        la = la_ref[sl, :]
        suf = jnp.dot(later, la, preferred_element_type=F32, precision=HIGHEST)
        tot = lax.dot_general(la, ones, (((0,), (0,)), ((), ())),
                              preferred_element_type=F32, precision=HIGHEST)
        k_dec = (k_ref[0, sl, :].astype(F32) * jnp.exp(suf)).astype(BF16)
        v = jnp.concatenate([v_ref[0, sl, :], v_ref[1, sl, :]], axis=-1)
        upd = lax.dot_general(k_dec, v, (((0,), (0,)), ((), ())),
                              preferred_element_type=F32)
        decay = jnp.exp(tot)
        decay = jnp.concatenate([decay] * (dv // LANES), axis=-1)
        st = decay * state_ref[...] + upd
        state_ref[...] = st
        ob_ref[sl, :] = jnp.dot(q_ref[0, sl, :], st.astype(BF16), preferred_element_type=F32)
        return carry

    lax.fori_loop(0, n_chunks, body, 0)

    o = ob_ref[...]
    ms = jnp.mean(o * o, axis=-1, keepdims=True)
    on = (o * lax.rsqrt(ms + RMS_EPS)) * ng_ref[...]
    r = jnp.concatenate([r_ref[0], r_ref[1]], axis=-1).astype(F32)
    o_ref[...] = (on * (r * jax.nn.sigmoid(r))).astype(BF16)


def _gla_scan(proj, gz, w_gate_up, b_gate, norm_g, *, batch, seq, lc=512):
    t = batch * seq
    lc = min(lc, seq)
    assert seq % lc == 0 and lc % CHUNK == 0
    ncb = seq // lc
    dk = SLAB
    dv = 2 * SLAB
    d_out = GLA_HEADS * dv
    rank = w_gate_up.shape[0]
    wg = jnp.zeros((LANES, GLA_HEADS * dk), F32).at[:rank].set(w_gate_up)

    def tok(b, h, c):
        return b * ncb + c

    return pl.pallas_call(
        functools.partial(_gla_kernel, n_chunks=lc // CHUNK),
        grid=(batch, GLA_HEADS, ncb),
        in_specs=[
            pl.BlockSpec((1, lc, SLAB), lambda b, h, c: (h, tok(b, h, c), 0)),
            pl.BlockSpec((1, lc, SLAB), lambda b, h, c: (GLA_HEADS + h, tok(b, h, c), 0)),
            pl.BlockSpec((2, lc, SLAB), lambda b, h, c: (GLA_HEADS + h, tok(b, h, c), 0)),
            pl.BlockSpec((2, lc, SLAB), lambda b, h, c: (2 * GLA_HEADS + h, tok(b, h, c), 0)),
            pl.BlockSpec((lc, LANES), lambda b, h, c: (tok(b, h, c), 0)),
            pl.BlockSpec((LANES, dk), lambda b, h, c: (0, h)),
            pl.BlockSpec((1, dk), lambda b, h, c: (0, h)),
            pl.BlockSpec((1, dv), lambda b, h, c: (0, 0)),
        ],
        out_specs=pl.BlockSpec((lc, dv), lambda b, h, c: (tok(b, h, c), h)),
        out_shape=jax.ShapeDtypeStruct((t, d_out), BF16),
        scratch_shapes=[pltpu.VMEM((dk, dv), F32), pltpu.VMEM((lc, dk), F32),
                        pltpu.VMEM((lc, dv), F32)],
        compiler_params=_cparams(("parallel", "parallel", "arbitrary")),
    )(proj, proj, proj, proj, gz, wg, b_gate.reshape(1, -1), norm_g.reshape(1, -1))


def _diff_attn_kernel(slope_ref, invct_ref, q_ref, k_ref, v_ref, qfeat_ref, kfeat_ref, lq1_ref, lk1_ref,
                      lq2_ref, lk2_ref, sg_ref, o_ref, vt_ref, dbias_ref, r0_ref, kmax_ref, m_ref, l_ref, acc_ref,
                      *, tq, lambda_init):
    h = pl.program_id(1)
    qi = pl.program_id(2)
    hd = SLAB // 2
    c = slope_ref[h] * LOG2E
    nt = (((1,), (1,)), ((), ()))

    ones16 = jnp.ones((16, hd), BF16)

    def sq_norms(x):
        xf = x.astype(F32)
        sq = (xf * xf * SQ_SLACK).astype(BF16)
        return lax.dot_general(ones16, sq, nt, preferred_element_type=F32)[0:1]

    @pl.when(qi == 0)
    def _():
        def per_block(i, carry):
            sl = pl.ds(pl.multiple_of(i * tq, tq), tq)
            vt_ref[:, sl] = v_ref[0, sl, :].T
            kb = k_ref[0, sl, :]
            return tuple(jnp.maximum(carry[u], sq_norms(kb[:, u * hd:(u + 1) * hd]))
                         for u in range(2))
        zero = jnp.zeros((1, tq), F32)
        kn2 = lax.fori_loop(0, v_ref.shape[1] // tq, per_block, (zero, zero))
        for u in range(2):
            kmax_ref[u] = jnp.broadcast_to(jnp.sqrt(jnp.max(kn2[u], axis=1, keepdims=True)),
                                           (8, LANES))
        ki = lax.broadcasted_iota(I32, (tq, tq), 0)
        qj = lax.broadcasted_iota(I32, (tq, tq), 1)
        bias = c * (qj - jnp.abs(qj - ki)).astype(F32)
        dbias_ref[...] = jnp.where((qj // CHUNK) >= (ki // CHUNK), bias, -jnp.inf)

    q = q_ref[0]
    start = pl.multiple_of(qi * tq, tq)

    k = k_ref[0, pl.ds(start, tq), :]
    vt = vt_ref[:, pl.ds(start, tq)]
    for u in range(2):
        r = lax.dot_general(k[:, u * hd:(u + 1) * hd], q[:, u * hd:(u + 1) * hd], nt,
                            preferred_element_type=F32) + dbias_ref[...]
        m_new = jnp.max(r, axis=0, keepdims=True)
        p = jnp.exp2(r - m_new)
        l_ref[u] = jnp.sum(p, axis=0, keepdims=True)
        acc_ref[u] = jnp.dot(vt, p.astype(BF16), preferred_element_type=F32)
        m_ref[u] = m_new

    qfeat = jnp.broadcast_to(qfeat_ref[0], (tq, LANES))
    kfeat = kfeat_ref[...]
    qa = [jnp.concatenate([q[:, u * hd:(u + 1) * hd], qfeat], axis=-1) for u in range(2)]

    def scores(u, blk):
        st = pl.multiple_of(blk * tq, tq)
        ka = jnp.concatenate([k_ref[0, pl.ds(st, tq), u * hd:(u + 1) * hd], kfeat], axis=-1)
        return lax.dot_general(ka, qa[u], nt, preferred_element_type=F32)

    def accumulate(u, r, vt, off):
        m_prev = m_ref[u]
        m_new = jnp.maximum(m_prev, jnp.max(r, axis=0, keepdims=True) - off)
        alpha = jnp.exp2(m_prev - m_new)
        p = jnp.exp2(r - (m_new + off))
        l_ref[u] = alpha * l_ref[u] + jnp.sum(p, axis=0, keepdims=True)
        acc_ref[u] = alpha * acc_ref[u] + jnp.dot(vt, p.astype(BF16),
                                                  preferred_element_type=F32)
        m_ref[u] = m_new

    r0_ref[...] = scores(0, jnp.maximum(qi - 1, 0))

    def past_block(dd, carry):
        blk = qi - dd
        vt = vt_ref[:, pl.ds(pl.multiple_of(blk * tq, tq), tq)]
        off = c * (dd * tq).astype(F32)
        r1 = scores(1, blk)
        accumulate(0, r0_ref[...], vt, off)
        r0_ref[...] = scores(0, jnp.maximum(blk - 1, 0))
        accumulate(1, r1, vt, off)
        return carry

    gap = None
    for u in range(2):
        bound = (jnp.sqrt(sq_norms(q[:, u * hd:(u + 1) * hd])) * kmax_ref[u][0:1, 0:1]
                 * NORM_SLACK - m_ref[u])
        g = jnp.max(bound, axis=1, keepdims=True)
        gap = g if gap is None else jnp.maximum(gap, g)
    n_need = jnp.ceil((gap + EXP_ZERO) * invct_ref[h])
    n_need = jnp.clip(n_need, 0.0, float(v_ref.shape[1] // tq)).astype(I32)[0, 0]
    lax.fori_loop(1, jnp.minimum(qi, n_need) + 1, past_block, 0)

    lam = (jnp.exp(jnp.sum(lq1_ref[...] * lk1_ref[...], axis=-1, keepdims=True))
           - jnp.exp(jnp.sum(lq2_ref[...] * lk2_ref[...], axis=-1, keepdims=True))
           + lambda_init)
    o = acc_ref[0] * (1.0 / l_ref[0]) - lam * (acc_ref[1] * (1.0 / l_ref[1]))
    o = o.T
    ms = jnp.mean(o * o, axis=-1, keepdims=True)
    o = (o * lax.rsqrt(ms + RMS_EPS)) * sg_ref[...]
    o_ref[...] = (o * (1.0 - lambda_init)).astype(BF16)


def _alibi_features(slopes, tq):
    c = slopes * LOG2E
    c_hi = c.astype(BF16).astype(F32)
    c_lo = c - c_hi
    qf = jnp.zeros((slopes.shape[0], 1, LANES), F32)
    qf = qf.at[:, 0, 0].set(LANES * c_hi).at[:, 0, 1].set(c_hi)
    qf = qf.at[:, 0, 2].set(LANES * c_lo).at[:, 0, 3].set(c_lo)
    idx = jnp.arange(tq, dtype=I32)
    hi = (idx // LANES).astype(F32)
    lo = (idx % LANES).astype(F32)
    kf = jnp.zeros((tq, LANES), F32).at[:, 0].set(hi).at[:, 1].set(lo).at[:, 2].set(hi).at[:, 3].set(lo)
    return qf.astype(BF16), kf.astype(BF16)


def _diff_attention(qkv, lq1, lk1, lq2, lk2, subln, *, batch, seq, layer_idx, tq=512):
    t = batch * seq
    tq = min(tq, seq)
    assert seq % tq == 0 and tq % CHUNK == 0
    nq = seq // tq
    lambda_init = 0.8 - 0.6 * math.exp(-0.3 * layer_idx)
    slopes = 2.0 ** (-8.0 * jnp.arange(1, DIFF_HEADS + 1, dtype=F32) / DIFF_HEADS)
    vec = lambda a: a.reshape(1, -1).astype(F32)
    qfeat, kfeat = _alibi_features(slopes, tq)
    grid_spec = pltpu.PrefetchScalarGridSpec(
        num_scalar_prefetch=2,
        grid=(batch, DIFF_HEADS, nq),
        in_specs=[
            pl.BlockSpec((1, tq, SLAB), lambda b, h, i, s, n: (h, b * nq + i, 0)),
            pl.BlockSpec((1, seq, SLAB), lambda b, h, i, s, n: (DIFF_HEADS + h, b, 0)),
            pl.BlockSpec((1, seq, SLAB), lambda b, h, i, s, n: (2 * DIFF_HEADS + h, b, 0)),
            pl.BlockSpec((1, 1, LANES), lambda b, h, i, s, n: (h, 0, 0)),
            pl.BlockSpec((tq, LANES), lambda b, h, i, s, n: (0, 0)),
            pl.BlockSpec((1, SLAB // 2), lambda b, h, i, s, n: (0, 0)),
            pl.BlockSpec((1, SLAB // 2), lambda b, h, i, s, n: (0, 0)),
            pl.BlockSpec((1, SLAB // 2), lambda b, h, i, s, n: (0, 0)),
            pl.BlockSpec((1, SLAB // 2), lambda b, h, i, s, n: (0, 0)),
            pl.BlockSpec((1, SLAB), lambda b, h, i, s, n: (0, 0)),
        ],
        out_specs=pl.BlockSpec((tq, SLAB), lambda b, h, i, s, n: (b * nq + i, h)),
        scratch_shapes=[pltpu.VMEM((SLAB, seq), BF16), pltpu.VMEM((tq, tq), F32),
                        pltpu.VMEM((tq, tq), F32), pltpu.VMEM((2, 8, LANES), F32),
                        pltpu.VMEM((2, 1, tq), F32), pltpu.VMEM((2, 1, tq), F32),
                        pltpu.VMEM((2, SLAB, tq), F32)],
    )
    return pl.pallas_call(
        functools.partial(_diff_attn_kernel, tq=tq, lambda_init=lambda_init),
        grid_spec=grid_spec,
        out_shape=jax.ShapeDtypeStruct((t, DIFF_HEADS * SLAB), BF16),
        compiler_params=_cparams(("parallel", "parallel", "arbitrary")),
    )(slopes, 1.0 / (slopes * (LOG2E * tq)), qkv, qkv, qkv, qfeat, kfeat, vec(lq1), vec(lk1), vec(lq2), vec(lk2), vec(subln))


def _proj_router_kernel(o_ref, w_ref, x_ref, g_ref, wr_ref, x1_ref, ri_ref, rw_ref):
    x1 = x_ref[...] + jnp.dot(o_ref[...], w_ref[...], preferred_element_type=F32)
    x1_ref[...] = x1
    ms = jnp.mean(x1 * x1, axis=-1, keepdims=True)
    hn = ((x1 * lax.rsqrt(ms + RMS_EPS)) * g_ref[...]).astype(BF16)
    logits = jnp.dot(hn, wr_ref[...], preferred_element_type=F32)
    lane = lax.broadcasted_iota(I32, logits.shape, 1)
    neg = -jnp.inf

    def first_max(vals):
        m = jnp.max(vals, axis=-1, keepdims=True)
        idx = jnp.min(jnp.where(vals == m, lane, LANES), axis=-1, keepdims=True)
        return m, idx

    gmask = lane < MOE_GROUPS
    gmax, gidx = first_max(jnp.where(gmask, logits, neg))
    gden = jnp.sum(jnp.where(gmask, jnp.exp(logits - gmax), 0.0), axis=-1, keepdims=True)
    g_w = 1.0 / gden
    lo = MOE_GROUPS + MOE_EXPERTS_PER_GROUP * gidx
    emask = (lane >= lo) & (lane < lo + MOE_EXPERTS_PER_GROUP)
    elog = jnp.where(emask, logits, neg)
    m1, i1 = first_max(elog)
    m2, i2 = first_max(jnp.where(lane == i1, neg, elog))
    tt = jnp.exp(m2 - m1)
    w1 = g_w / (1.0 + tt)
    w2 = w1 * tt
    ri_ref[...] = jnp.where(lane == 0, i1 - MOE_GROUPS, jnp.where(lane == 1, i2 - MOE_GROUPS, 0))
    rw_ref[...] = jnp.where(lane == 0, w1, jnp.where(lane == 1, w2, 0.0))


def _proj_router(o, w_out, x, g_ffn, w_router, *, tm=512):
    t, d = x.shape
    tm = min(tm, t)
    assert t % tm == 0
    return pl.pallas_call(
        _proj_router_kernel,
        grid=(t // tm,),
        in_specs=[
            pl.BlockSpec((tm, d), lambda i: (i, 0)),
            pl.BlockSpec((d, d), lambda i: (0, 0)),
            pl.BlockSpec((tm, d), lambda i: (i, 0)),
            pl.BlockSpec((1, d), lambda i: (0, 0)),
            pl.BlockSpec((d, LANES), lambda i: (0, 0)),
        ],
        out_specs=[
            pl.BlockSpec((tm, d), lambda i: (i, 0)),
            pl.BlockSpec((tm, LANES), lambda i: (i, 0)),
            pl.BlockSpec((tm, LANES), lambda i: (i, 0)),
        ],
        out_shape=[jax.ShapeDtypeStruct((t, d), F32), jax.ShapeDtypeStruct((t, LANES), I32),
                   jax.ShapeDtypeStruct((t, LANES), F32)],
        compiler_params=_cparams(("parallel",)),
    )(o, w_out, x, g_ffn.reshape(1, d), w_router)


def _gather_rows(idx_ref, n, src_hbm, dst_ref, sem):
    def body(r, carry):
        t = idx_ref[0, 0, r]
        pltpu.make_async_copy(src_hbm.at[pl.ds(t, 1), :], dst_ref.at[pl.ds(r, 1), :], sem).start()
        return carry
    lax.fori_loop(0, n, body, 0, unroll=8)


def _wait_rows(n, src_hbm, dst_ref, sem):
    pltpu.make_async_copy(src_hbm.at[pl.ds(0, n), :], dst_ref, sem).wait()


def _moe_kernel(eid_ref, nused_ref, tok_ref, tok_next_ref, x_hbm, g_ref, wg32_ref, wu32_ref,
                wd32_ref, o_ref, xbuf, sem, wg_ref, wu_ref, wd_ref, *, blk):
    b = pl.program_id(0)
    nb = pl.num_programs(0)
    slot = b % 2

    @pl.when((b == 0) | (eid_ref[b] != eid_ref[jnp.maximum(b - 1, 0)]))
    def _():
        wg_ref[0] = wg32_ref[0].astype(BF16)
        wu_ref[0] = wu32_ref[0].astype(BF16)
        wd_ref[0] = wd32_ref[0].astype(BF16)

    @pl.when(b == 0)
    def _():
        _gather_rows(tok_ref, blk, x_hbm, xbuf.at[0], sem.at[0])

    @pl.when(b + 1 < nb)
    def _():
        _gather_rows(tok_next_ref, blk, x_hbm, xbuf.at[1 - slot], sem.at[1 - slot])

    _wait_rows(blk, x_hbm, xbuf.at[slot], sem.at[slot])

    @pl.when(b < nused_ref[0])
    def _():
        x = xbuf[slot]
        ms = jnp.mean(x * x, axis=-1, keepdims=True)
        hn = ((x * lax.rsqrt(ms + RMS_EPS)) * g_ref[...]).astype(BF16)
        hg = jnp.dot(hn, wg_ref[0], preferred_element_type=F32)
        hu = jnp.dot(hn, wu_ref[0], preferred_element_type=F32)
        act = ((hg * jax.nn.sigmoid(hg)) * hu).astype(BF16)
        o_ref[...] = jnp.dot(act, wd_ref[0], preferred_element_type=F32)

    @pl.when(b >= nused_ref[0])
    def _():
        o_ref[...] = jnp.zeros_like(o_ref)


def _moe_ffn(x, g_ffn, blk_eid, n_used, row_tok, w_gate, w_up, w_down, *, blk):
    t, d = x.shape
    n_rows = row_tok.shape[0]
    n_blocks = n_rows // blk
    dff = w_gate.shape[-1]
    tok3 = row_tok.reshape(n_blocks, 1, blk)
    grid_spec = pltpu.PrefetchScalarGridSpec(
        num_scalar_prefetch=2,
        grid=(n_blocks,),
        in_specs=[
            pl.BlockSpec((1, 1, blk), lambda b, e, u: (b, 0, 0), memory_space=pltpu.SMEM),
            pl.BlockSpec((1, 1, blk), lambda b, e, u: (jnp.minimum(b + 1, n_blocks - 1), 0, 0),
                         memory_space=pltpu.SMEM),
            pl.BlockSpec(memory_space=pl.ANY),
            pl.BlockSpec((1, d), lambda b, e, u: (0, 0)),
            pl.BlockSpec((1, d, dff), lambda b, e, u: (e[b], 0, 0)),
            pl.BlockSpec((1, d, dff), lambda b, e, u: (e[b], 0, 0)),
            pl.BlockSpec((1, dff, d), lambda b, e, u: (e[b], 0, 0)),
        ],
        out_specs=pl.BlockSpec((blk, d), lambda b, e, u: (b, 0)),
        scratch_shapes=[pltpu.VMEM((2, blk, d), F32), pltpu.SemaphoreType.DMA((2,)),
                        pltpu.VMEM((1, d, dff), BF16), pltpu.VMEM((1, d, dff), BF16),
                        pltpu.VMEM((1, dff, d), BF16)],
    )
    return pl.pallas_call(
        functools.partial(_moe_kernel, blk=blk),
        grid_spec=grid_spec,
        out_shape=jax.ShapeDtypeStruct((n_rows, d), F32),
        compiler_params=_cparams(("arbitrary",)),
    )(blk_eid, n_used, tok3, tok3, x, g_ffn.reshape(1, d), w_gate, w_up, w_down)


def _combine_kernel(pos_ref, pos_next_ref, x_ref, rw_ref, y_hbm, g_ref, o_ref, ybuf, sem,
                    *, tc, final_norm):
    i = pl.program_id(0)
    n = pl.num_programs(0)
    slot = i % 2

    @pl.when(i == 0)
    def _():
        _gather_rows(pos_ref, 2 * tc, y_hbm, ybuf.at[0], sem.at[0])

    @pl.when(i + 1 < n)
    def _():
        _gather_rows(pos_next_ref, 2 * tc, y_hbm, ybuf.at[1 - slot], sem.at[1 - slot])

    _wait_rows(2 * tc, y_hbm, ybuf.at[slot], sem.at[slot])

    rw = rw_ref[...]
    y = x_ref[...] + rw[:, 0:1] * ybuf[slot, pl.ds(0, tc), :] + rw[:, 1:2] * ybuf[slot, pl.ds(tc, tc), :]
    if final_norm:
        ms = jnp.mean(y * y, axis=-1, keepdims=True)
        y = (y * lax.rsqrt(ms + RMS_EPS)) * g_ref[...]
    o_ref[...] = y


def _combine(x, route_w, pos, y_sorted, g_final, *, final_norm, tc=256):
    t, d = x.shape
    tc = min(tc, t)
    assert t % tc == 0
    nt = t // tc
    pos3 = pos.reshape(nt, tc, 2).transpose(0, 2, 1).reshape(nt, 1, 2 * tc)
    return pl.pallas_call(
        functools.partial(_combine_kernel, tc=tc, final_norm=final_norm),
        grid=(nt,),
        in_specs=[
            pl.BlockSpec((1, 1, 2 * tc), lambda i: (i, 0, 0), memory_space=pltpu.SMEM),
            pl.BlockSpec((1, 1, 2 * tc), lambda i: (jnp.minimum(i + 1, nt - 1), 0, 0),
                         memory_space=pltpu.SMEM),
            pl.BlockSpec((tc, d), lambda i: (i, 0)),
            pl.BlockSpec((tc, LANES), lambda i: (i, 0)),
            pl.BlockSpec(memory_space=pl.ANY),
            pl.BlockSpec((1, d), lambda i: (0, 0)),
        ],
        out_specs=pl.BlockSpec((tc, d), lambda i: (i, 0)),
        out_shape=jax.ShapeDtypeStruct((t, d), F32),
        scratch_shapes=[pltpu.VMEM((2, 2 * tc, d), F32), pltpu.SemaphoreType.DMA((2,))],
        compiler_params=_cparams(("arbitrary",)),
    )(pos3, pos3, x, route_w, y_sorted, g_final.reshape(1, d))


def _dispatch(route_i, blk):
    t = route_i.shape[0]
    eid = route_i[:, :2].reshape(-1)
    n_assign = eid.shape[0]
    onehot = (eid[:, None] == jnp.arange(MOE_N_EXPERTS, dtype=I32)[None, :]).astype(F32)
    tile = 256
    oh = onehot.reshape(n_assign // tile, tile, MOE_N_EXPERTS)
    within = jnp.einsum('ij,tje->tie', jnp.tril(jnp.ones((tile, tile), F32)), oh)
    tile_tot = within[:, -1, :]
    tile_off = jnp.cumsum(tile_tot, axis=0) - tile_tot
    csum = within + tile_off[:, None, :]
    rank = (jnp.sum(csum * oh, axis=2) - 1.0).reshape(n_assign).astype(I32)
    counts = (tile_off[-1] + tile_tot[-1]).astype(I32)
    padded = ((counts + blk - 1) // blk) * blk
    pend = jnp.cumsum(padded)
    pstart = pend - padded
    dest = (pstart[eid] + rank).astype(I32)
    n_rows = n_assign + MOE_N_EXPERTS * blk
    n_blocks = n_rows // blk
    row_tok = jnp.zeros((n_rows,), I32).at[dest].set(jnp.arange(n_assign, dtype=I32) // 2)
    blk_start = jnp.arange(n_blocks, dtype=I32) * blk
    blk_eid = jnp.minimum(jnp.sum((pend[None, :] <= blk_start[:, None]).astype(I32), axis=1),
                          MOE_N_EXPERTS - 1)
    n_used = (pend[-1:] // blk).astype(I32)
    return row_tok, blk_eid, n_used, dest.reshape(t, 2)


def _router_weights(w_group, w_expert):
    d = w_group.shape[0]
    w = jnp.zeros((d, LANES), F32)
    w = w.at[:, :MOE_GROUPS].set(w_group).at[:, MOE_GROUPS:MOE_GROUPS + MOE_N_EXPERTS].set(w_expert)
    return w.astype(BF16)


def _moe_layer(x1, route_i, route_w, g_ffn, w_gate, w_up, w_down, g_final, *, final_norm, blk=256):
    row_tok, blk_eid, n_used, pos = _dispatch(route_i, blk)
    y_sorted = _moe_ffn(x1, g_ffn, blk_eid, n_used, row_tok, w_gate, w_up, w_down, blk=blk)
    return _combine(x1, route_w, pos, y_sorted, g_final, final_norm=final_norm)


def kernel(x, norm_mix, norm_ffn, gla_w_in, gla_w_gate_up, gla_b_gate, gla_norm, gla_w_out,
           diff_w_in, diff_lambda_q1, diff_lambda_k1, diff_lambda_q2, diff_lambda_k2,
           diff_subln, diff_w_out, moe_w_group, moe_w_expert, moe_w_gate, moe_w_up,
           moe_w_down, final_norm):
    batch, seq, d = x.shape
    t = batch * seq
    xf = x.reshape(t, d)

    dk_total = d // 2
    n_main = 2 * dk_total + 2 * d
    w_in = gla_w_in[0]
    gla_dk = dk_total // GLA_HEADS
    colscale = jnp.ones((n_main,), F32).at[:dk_total].set(gla_dk ** -0.5)
    rank = w_in.shape[1] - n_main
    w_gz = jnp.zeros((d, LANES), F32).at[:, :rank].set(w_in[:, n_main:]).astype(BF16)
    proj, gz = _norm_matmul(xf, norm_mix[0], w_in[:, :n_main].astype(BF16), colscale, w_gz)
    o = _gla_scan(proj, gz, gla_w_gate_up[0], gla_b_gate[0], gla_norm[0], batch=batch, seq=seq)
    x1, ri, rw = _proj_router(o, gla_w_out[0].astype(BF16), xf, norm_ffn[0],
                              _router_weights(moe_w_group[0], moe_w_expert[0]))
    x2 = _moe_layer(x1, ri, rw, norm_ffn[0], moe_w_gate[0], moe_w_up[0], moe_w_down[0],
                    final_norm, final_norm=False)

    hd = d // DIFF_HEADS // 2
    colscale = jnp.ones((3 * d,), F32).at[:d].set(hd ** -0.5 * LOG2E)
    qkv = _norm_matmul(x2, norm_mix[1], diff_w_in[0].astype(BF16), colscale)
    o = _diff_attention(qkv, diff_lambda_q1[0], diff_lambda_k1[0], diff_lambda_q2[0],
                        diff_lambda_k2[0], diff_subln[0], batch=batch, seq=seq, layer_idx=1)
    x3, ri, rw = _proj_router(o, diff_w_out[0].astype(BF16), x2, norm_ffn[1],
                              _router_weights(moe_w_group[1], moe_w_expert[1]))
    out = _moe_layer(x3, ri, rw, norm_ffn[1], moe_w_gate[1], moe_w_up[1], moe_w_down[1],
                     final_norm, final_norm=True)
    return out.reshape(batch, seq, d)
```

```python
import functools
import math

import jax
import jax.numpy as jnp
from jax import lax
from jax.experimental import pallas as pl
from jax.experimental.pallas import tpu as pltpu

F32 = jnp.float32
BF16 = jnp.bfloat16
I32 = jnp.int32

RMS_EPS = 1e-6
CHUNK = 64
GLA_HEADS = 4
GLA_GATE_TAU = 16.0
DIFF_HEADS = 8
MOE_GROUPS = 4
MOE_EXPERTS_PER_GROUP = 8
MOE_N_EXPERTS = MOE_GROUPS * MOE_EXPERTS_PER_GROUP

LANES = 128
SLAB = 256
VMEM_LIMIT = 56 * 1024 * 1024
LOG2E = 1.4426950408889634
NEG_BIG = -1e30
EXP_ZERO = 160.0
NORM_SLACK = 1.0 + 2.0 ** -10
SQ_SLACK = 1.0 + 2.0 ** -7

HIGHEST = lax.Precision.HIGHEST


def _cparams(sem):
    return pltpu.CompilerParams(dimension_semantics=sem, vmem_limit_bytes=VMEM_LIMIT)


def _norm_matmul_kernel(x_ref, g_ref, w_ref, cs_ref, *rest, n_slab, has_aux):
    if has_aux:
        w2_ref, o_ref, o2_ref, hn_ref = rest
    else:
        o_ref, hn_ref = rest

    @pl.when(pl.program_id(1) == 0)
    def _():
        x = x_ref[...]
        ms = jnp.mean(x * x, axis=-1, keepdims=True)
        hn = ((x * lax.rsqrt(ms + RMS_EPS)) * g_ref[...]).astype(BF16)
        hn_ref[...] = hn
        if has_aux:
            o2_ref[...] = jnp.dot(hn, w2_ref[...], preferred_element_type=F32)

    acc = jnp.dot(hn_ref[...], w_ref[...], preferred_element_type=F32) * cs_ref[...]
    for s in range(n_slab):
        o_ref[s] = acc[:, s * SLAB:(s + 1) * SLAB].astype(BF16)


def _norm_matmul(x, g, w, colscale, w_aux=None, *, tm=1024, tn=1024):
    t, d = x.shape
    n = w.shape[1]
    tm = min(tm, t)
    assert t % tm == 0 and n % tn == 0 and tn % SLAB == 0
    n_slab = tn // SLAB
    has_aux = w_aux is not None
    in_specs = [
        pl.BlockSpec((tm, d), lambda i, j: (i, 0)),
        pl.BlockSpec((1, d), lambda i, j: (0, 0)),
        pl.BlockSpec((d, tn), lambda i, j: (0, j)),
        pl.BlockSpec((1, tn), lambda i, j: (0, j)),
    ]
    out_shape = [jax.ShapeDtypeStruct((n // SLAB, t, SLAB), BF16)]
    out_specs = [pl.BlockSpec((n_slab, tm, SLAB), lambda i, j: (j, i, 0))]
    args = [x, g.reshape(1, d), w, colscale.reshape(1, n)]
    if has_aux:
        in_specs.append(pl.BlockSpec((d, LANES), lambda i, j: (0, 0)))
        out_shape.append(jax.ShapeDtypeStruct((t, LANES), F32))
        out_specs.append(pl.BlockSpec((tm, LANES), lambda i, j: (i, 0)))
        args.append(w_aux)
    res = pl.pallas_call(
        functools.partial(_norm_matmul_kernel, n_slab=n_slab, has_aux=has_aux),
        grid=(t // tm, n // tn),
        in_specs=in_specs,
        out_specs=out_specs,
        out_shape=out_shape,
        scratch_shapes=[pltpu.VMEM((tm, d), BF16)],
        compiler_params=_cparams(("parallel", "arbitrary")),
    )(*args)
    return res if has_aux else res[0]


def _gla_kernel(q_ref, k_ref, v_ref, r_ref, gz_ref, wg_ref, bg_ref, ng_ref, o_ref,
                state_ref, la_ref, ob_ref, *, n_chunks):
    @pl.when(pl.program_id(2) == 0)
    def _():
        state_ref[...] = jnp.zeros_like(state_ref)

    z = jnp.dot(gz_ref[...], wg_ref[...], preferred_element_type=F32,
                precision=HIGHEST) + bg_ref[...]
    la_ref[...] = -(jnp.maximum(-z, 0.0) + jnp.log1p(jnp.exp(-jnp.abs(z)))) * (1.0 / GLA_GATE_TAU)

    row = lax.broadcasted_iota(I32, (CHUNK, CHUNK), 0)
    col = lax.broadcasted_iota(I32, (CHUNK, CHUNK), 1)
    later = (col > row).astype(F32)
    ones = jnp.ones((CHUNK, LANES), F32)
    dv = state_ref.shape[1]

    def body(i, carry):
        sl = pl.ds(pl.multiple_of(i * CHUNK, CHUNK), CHUNK)
        la = la_ref[sl, :]
        suf = jnp.dot(later, la, preferred_element_type=F32, precision=HIGHEST)
        tot = lax.dot_general(la, ones, (((0,), (0,)), ((), ())),
                              preferred_element_type=F32, precision=HIGHEST)
        k_dec = (k_ref[0, sl, :].astype(F32) * jnp.exp(suf)).astype(BF16)
        v = jnp.concatenate([v_ref[0, sl, :], v_ref[1, sl, :]], axis=-1)
        upd = lax.dot_general(k_dec, v, (((0,), (0,)), ((), ())),
                              preferred_element_type=F32)
        decay = jnp.exp(tot)
        decay = jnp.concatenate([decay] * (dv // LANES), axis=-1)
        st = decay * state_ref[...] + upd
        state_ref[...] = st
        ob_ref[sl, :] = jnp.dot(q_ref[0, sl, :], st.astype(BF16), preferred_element_type=F32)
        return carry

    lax.fori_loop(0, n_chunks, body, 0)

    o = ob_ref[...]
    ms = jnp.mean(o * o, axis=-1, keepdims=True)
    on = (o * lax.rsqrt(ms + RMS_EPS)) * ng_ref[...]
    r = jnp.concatenate([r_ref[0], r_ref[1]], axis=-1).astype(F32)
    o_ref[...] = (on * (r * jax.nn.sigmoid(r))).astype(BF16)


def _gla_scan(proj, gz, w_gate_up, b_gate, norm_g, *, batch, seq, lc=512):
    t = batch * seq
    lc = min(lc, seq)
    assert seq % lc == 0 and lc % CHUNK == 0
    ncb = seq // lc
    dk = SLAB
    dv = 2 * SLAB
    d_out = GLA_HEADS * dv
    rank = w_gate_up.shape[0]
    wg = jnp.zeros((LANES, GLA_HEADS * dk), F32).at[:rank].set(w_gate_up)

    def tok(b, h, c):
        return b * ncb + c

    return pl.pallas_call(
        functools.partial(_gla_kernel, n_chunks=lc // CHUNK),
        grid=(batch, GLA_HEADS, ncb),
        in_specs=[
            pl.BlockSpec((1, lc, SLAB), lambda b, h, c: (h, tok(b, h, c), 0)),
            pl.BlockSpec((1, lc, SLAB), lambda b, h, c: (GLA_HEADS + h, tok(b, h, c), 0)),
            pl.BlockSpec((2, lc, SLAB), lambda b, h, c: (GLA_HEADS + h, tok(b, h, c), 0)),
            pl.BlockSpec((2, lc, SLAB), lambda b, h, c: (2 * GLA_HEADS + h, tok(b, h, c), 0)),
            pl.BlockSpec((lc, LANES), lambda b, h, c: (tok(b, h, c), 0)),
            pl.BlockSpec((LANES, dk), lambda b, h, c: (0, h)),
            pl.BlockSpec((1, dk), lambda b, h, c: (0, h)),
            pl.BlockSpec((1, dv), lambda b, h, c: (0, 0)),
        ],
        out_specs=pl.BlockSpec((lc, dv), lambda b, h, c: (tok(b, h, c), h)),
        out_shape=jax.ShapeDtypeStruct((t, d_out), BF16),
        scratch_shapes=[pltpu.VMEM((dk, dv), F32), pltpu.VMEM((lc, dk), F32),
                        pltpu.VMEM((lc, dv), F32)],
        compiler_params=_cparams(("parallel", "parallel", "arbitrary")),
    )(proj, proj, proj, proj, gz, wg, b_gate.reshape(1, -1), norm_g.reshape(1, -1))


def _diff_attn_kernel(slope_ref, invct_ref, q_ref, k_ref, v_ref, qfeat_ref, kfeat_ref, lq1_ref, lk1_ref,
                      lq2_ref, lk2_ref, sg_ref, o_ref, vt_ref, dbias_ref, r0_ref, kmax_ref, m_ref, l_ref, acc_ref,
                      *, tq, lambda_init):
    h = pl.program_id(1)
    qi = pl.program_id(2)
    hd = SLAB // 2
    c = slope_ref[h] * LOG2E
    nt = (((1,), (1,)), ((), ()))

    ones16 = jnp.ones((16, hd), BF16)

    def sq_norms(x):
        xf = x.astype(F32)
        sq = (xf * xf * SQ_SLACK).astype(BF16)
        return lax.dot_general(ones16, sq, nt, preferred_element_type=F32)[0:1]

    @pl.when(qi == 0)
    def _():
        def per_block(i, carry):
            sl = pl.ds(pl.multiple_of(i * tq, tq), tq)
            vt_ref[:, sl] = v_ref[0, sl, :].T
            kb = k_ref[0, sl, :]
            return tuple(jnp.maximum(carry[u], sq_norms(kb[:, u * hd:(u + 1) * hd]))
                         for u in range(2))
        zero = jnp.zeros((1, tq), F32)
        kn2 = lax.fori_loop(0, v_ref.shape[1] // tq, per_block, (zero, zero))
        for u in range(2):
            kmax_ref[u] = jnp.broadcast_to(jnp.sqrt(jnp.max(kn2[u], axis=1, keepdims=True)),
                                           (8, LANES))
        ki = lax.broadcasted_iota(I32, (tq, tq), 0)
        qj = lax.broadcasted_iota(I32, (tq, tq), 1)
        bias = c * (qj - jnp.abs(qj - ki)).astype(F32)
        dbias_ref[...] = jnp.where((qj // CHUNK) >= (ki // CHUNK), bias, -jnp.inf)

    q = q_ref[0]
    start = pl.multiple_of(qi * tq, tq)

    k = k_ref[0, pl.ds(start, tq), :]
    vt = vt_ref[:, pl.ds(start, tq)]
    for u in range(2):
        r = lax.dot_general(k[:, u * hd:(u + 1) * hd], q[:, u * hd:(u + 1) * hd], nt,
                            preferred_element_type=F32) + dbias_ref[...]
        m_new = jnp.max(r, axis=0, keepdims=True)
        p = jnp.exp2(r - m_new)
        l_ref[u] = jnp.sum(p, axis=0, keepdims=True)
        acc_ref[u] = jnp.dot(vt, p.astype(BF16), preferred_element_type=F32)
        m_ref[u] = m_new

    qfeat = jnp.broadcast_to(qfeat_ref[0], (tq, LANES))
    kfeat = kfeat_ref[...]
    qa = [jnp.concatenate([q[:, u * hd:(u + 1) * hd], qfeat], axis=-1) for u in range(2)]

    def scores(u, blk):
        st = pl.multiple_of(blk * tq, tq)
        ka = jnp.concatenate([k_ref[0, pl.ds(st, tq), u * hd:(u + 1) * hd], kfeat], axis=-1)
        return lax.dot_general(ka, qa[u], nt, preferred_element_type=F32)

    def accumulate(u, r, vt, off):
        m_prev = m_ref[u]
        m_new = jnp.maximum(m_prev, jnp.max(r, axis=0, keepdims=True) - off)
        alpha = jnp.exp2(m_prev - m_new)
        p = jnp.exp2(r - (m_new + off))
        l_ref[u] = alpha * l_ref[u] + jnp.sum(p, axis=0, keepdims=True)
        acc_ref[u] = alpha * acc_ref[u] + jnp.dot(vt, p.astype(BF16),
                                                  preferred_element_type=F32)
        m_ref[u] = m_new

    r0_ref[...] = scores(0, jnp.maximum(qi - 1, 0))

    def past_block(dd, carry):
        blk = qi - dd
        vt = vt_ref[:, pl.ds(pl.multiple_of(blk * tq, tq), tq)]
        off = c * (dd * tq).astype(F32)
        r1 = scores(1, blk)
        accumulate(0, r0_ref[...], vt, off)
        r0_ref[...] = scores(0, jnp.maximum(blk - 1, 0))
        accumulate(1, r1, vt, off)
        return carry

    gap = None
    for u in range(2):
        bound = (jnp.sqrt(sq_norms(q[:, u * hd:(u + 1) * hd])) * kmax_ref[u][0:1, 0:1]
                 * NORM_SLACK - m_ref[u])
        g = jnp.max(bound, axis=1, keepdims=True)
        gap = g if gap is None else jnp.maximum(gap, g)
    n_need = jnp.ceil((gap + EXP_ZERO) * invct_ref[h])
    n_need = jnp.clip(n_need, 0.0, float(v_ref.shape[1] // tq)).astype(I32)[0, 0]
    lax.fori_loop(1, jnp.minimum(qi, n_need) + 1, past_block, 0)

    lam = (jnp.exp(jnp.sum(lq1_ref[...] * lk1_ref[...], axis=-1, keepdims=True))
           - jnp.exp(jnp.sum(lq2_ref[...] * lk2_ref[...], axis=-1, keepdims=True))
           + lambda_init)
    o = acc_ref[0] * (1.0 / l_ref[0]) - lam * (acc_ref[1] * (1.0 / l_ref[1]))
    o = o.T
    ms = jnp.mean(o * o, axis=-1, keepdims=True)
    o = (o * lax.rsqrt(ms + RMS_EPS)) * sg_ref[...]
    o_ref[...] = (o * (1.0 - lambda_init)).astype(BF16)


def _alibi_features(slopes, tq):
    c = slopes * LOG2E
    c_hi = c.astype(BF16).astype(F32)
    c_lo = c - c_hi
    qf = jnp.zeros((slopes.shape[0], 1, LANES), F32)
    qf = qf.at[:, 0, 0].set(LANES * c_hi).at[:, 0, 1].set(c_hi)
    qf = qf.at[:, 0, 2].set(LANES * c_lo).at[:, 0, 3].set(c_lo)
    idx = jnp.arange(tq, dtype=I32)
    hi = (idx // LANES).astype(F32)
    lo = (idx % LANES).astype(F32)
    kf = jnp.zeros((tq, LANES), F32).at[:, 0].set(hi).at[:, 1].set(lo).at[:, 2].set(hi).at[:, 3].set(lo)
    return qf.astype(BF16), kf.astype(BF16)


def _diff_attention(qkv, lq1, lk1, lq2, lk2, subln, *, batch, seq, layer_idx, tq=512):
    t = batch * seq
    tq = min(tq, seq)
    assert seq % tq == 0 and tq % CHUNK == 0
    nq = seq // tq
    lambda_init = 0.8 - 0.6 * math.exp(-0.3 * layer_idx)
    slopes = 2.0 ** (-8.0 * jnp.arange(1, DIFF_HEADS + 1, dtype=F32) / DIFF_HEADS)
    vec = lambda a: a.reshape(1, -1).astype(F32)
    qfeat, kfeat = _alibi_features(slopes, tq)
    grid_spec = pltpu.PrefetchScalarGridSpec(
        num_scalar_prefetch=2,
        grid=(batch, DIFF_HEADS, nq),
        in_specs=[
            pl.BlockSpec((1, tq, SLAB), lambda b, h, i, s, n: (h, b * nq + i, 0)),
            pl.BlockSpec((1, seq, SLAB), lambda b, h, i, s, n: (DIFF_HEADS + h, b, 0)),
            pl.BlockSpec((1, seq, SLAB), lambda b, h, i, s, n: (2 * DIFF_HEADS + h, b, 0)),
            pl.BlockSpec((1, 1, LANES), lambda b, h, i, s, n: (h, 0, 0)),
            pl.BlockSpec((tq, LANES), lambda b, h, i, s, n: (0, 0)),
            pl.BlockSpec((1, SLAB // 2), lambda b, h, i, s, n: (0, 0)),
            pl.BlockSpec((1, SLAB // 2), lambda b, h, i, s, n: (0, 0)),
            pl.BlockSpec((1, SLAB // 2), lambda b, h, i, s, n: (0, 0)),
            pl.BlockSpec((1, SLAB // 2), lambda b, h, i, s, n: (0, 0)),
            pl.BlockSpec((1, SLAB), lambda b, h, i, s, n: (0, 0)),
        ],
        out_specs=pl.BlockSpec((tq, SLAB), lambda b, h, i, s, n: (b * nq + i, h)),
        scratch_shapes=[pltpu.VMEM((SLAB, seq), BF16), pltpu.VMEM((tq, tq), F32),
                        pltpu.VMEM((tq, tq), F32), pltpu.VMEM((2, 8, LANES), F32),
                        pltpu.VMEM((2, 1, tq), F32), pltpu.VMEM((2, 1, tq), F32),
                        pltpu.VMEM((2, SLAB, tq), F32)],
    )
    return pl.pallas_call(
        functools.partial(_diff_attn_kernel, tq=tq, lambda_init=lambda_init),
        grid_spec=grid_spec,
        out_shape=jax.ShapeDtypeStruct((t, DIFF_HEADS * SLAB), BF16),
        compiler_params=_cparams(("parallel", "parallel", "arbitrary")),
    )(slopes, 1.0 / (slopes * (LOG2E * tq)), qkv, qkv, qkv, qfeat, kfeat, vec(lq1), vec(lk1), vec(lq2), vec(lk2), vec(subln))


def _proj_router_kernel(o_ref, w_ref, x_ref, g_ref, wr_ref, x1_ref, ri_ref, rw_ref):
    x1 = x_ref[...] + jnp.dot(o_ref[...], w_ref[...], preferred_element_type=F32)
    x1_ref[...] = x1
    ms = jnp.mean(x1 * x1, axis=-1, keepdims=True)
    hn = ((x1 * lax.rsqrt(ms + RMS_EPS)) * g_ref[...]).astype(BF16)
    logits = jnp.dot(hn, wr_ref[...], preferred_element_type=F32)
    lane = lax.broadcasted_iota(I32, logits.shape, 1)
    neg = -jnp.inf

    def first_max(vals):
        m = jnp.max(vals, axis=-1, keepdims=True)
        idx = jnp.min(jnp.where(vals == m, lane, LANES), axis=-1, keepdims=True)
        return m, idx

    gmask = lane < MOE_GROUPS
    gmax, gidx = first_max(jnp.where(gmask, logits, neg))
    gden = jnp.sum(jnp.where(gmask, jnp.exp(logits - gmax), 0.0), axis=-1, keepdims=True)
    g_w = 1.0 / gden
    lo = MOE_GROUPS + MOE_EXPERTS_PER_GROUP * gidx
    emask = (lane >= lo) & (lane < lo + MOE_EXPERTS_PER_GROUP)
    elog = jnp.where(emask, logits, neg)
    m1, i1 = first_max(elog)
    m2, i2 = first_max(jnp.where(lane == i1, neg, elog))
    tt = jnp.exp(m2 - m1)
    w1 = g_w / (1.0 + tt)
    w2 = w1 * tt
    ri_ref[...] = jnp.where(lane == 0, i1 - MOE_GROUPS, jnp.where(lane == 1, i2 - MOE_GROUPS, 0))
    rw_ref[...] = jnp.where(lane == 0, w1, jnp.where(lane == 1, w2, 0.0))


def _proj_router(o, w_out, x, g_ffn, w_router, *, tm=512):
    t, d = x.shape
    tm = min(tm, t)
    assert t % tm == 0
    return pl.pallas_call(
        _proj_router_kernel,
        grid=(t // tm,),
        in_specs=[
            pl.BlockSpec((tm, d), lambda i: (i, 0)),
            pl.BlockSpec((d, d), lambda i: (0, 0)),
            pl.BlockSpec((tm, d), lambda i: (i, 0)),
            pl.BlockSpec((1, d), lambda i: (0, 0)),
            pl.BlockSpec((d, LANES), lambda i: (0, 0)),
        ],
        out_specs=[
            pl.BlockSpec((tm, d), lambda i: (i, 0)),
            pl.BlockSpec((tm, LANES), lambda i: (i, 0)),
            pl.BlockSpec((tm, LANES), lambda i: (i, 0)),
        ],
        out_shape=[jax.ShapeDtypeStruct((t, d), F32), jax.ShapeDtypeStruct((t, LANES), I32),
                   jax.ShapeDtypeStruct((t, LANES), F32)],
        compiler_params=_cparams(("parallel",)),
    )(o, w_out, x, g_ffn.reshape(1, d), w_router)


def _gather_rows(idx_ref, n, src_hbm, dst_ref, sem):
    def body(r, carry):
        t = idx_ref[0, 0, r]
        pltpu.make_async_copy(src_hbm.at[pl.ds(t, 1), :], dst_ref.at[pl.ds(r, 1), :], sem).start()
        return carry
    lax.fori_loop(0, n, body, 0, unroll=8)


def _wait_rows(n, src_hbm, dst_ref, sem):
    pltpu.make_async_copy(src_hbm.at[pl.ds(0, n), :], dst_ref, sem).wait()


def _moe_kernel(eid_ref, nused_ref, tok_ref, tok_next_ref, x_hbm, g_ref, wg32_ref, wu32_ref,
                wd32_ref, o_ref, xbuf, sem, wg_ref, wu_ref, wd_ref, *, blk):
    b = pl.program_id(0)
    nb = pl.num_programs(0)
    slot = b % 2

    @pl.when((b == 0) | (eid_ref[b] != eid_ref[jnp.maximum(b - 1, 0)]))
    def _():
        wg_ref[0] = wg32_ref[0].astype(BF16)
        wu_ref[0] = wu32_ref[0].astype(BF16)
        wd_ref[0] = wd32_ref[0].astype(BF16)

    @pl.when(b == 0)
    def _():
        _gather_rows(tok_ref, blk, x_hbm, xbuf.at[0], sem.at[0])

    @pl.when(b + 1 < nb)
    def _():
        _gather_rows(tok_next_ref, blk, x_hbm, xbuf.at[1 - slot], sem.at[1 - slot])

    _wait_rows(blk, x_hbm, xbuf.at[slot], sem.at[slot])

    @pl.when(b < nused_ref[0])
    def _():
        x = xbuf[slot]
        ms = jnp.mean(x * x, axis=-1, keepdims=True)
        hn = ((x * lax.rsqrt(ms + RMS_EPS)) * g_ref[...]).astype(BF16)
        hg = jnp.dot(hn, wg_ref[0], preferred_element_type=F32)
        hu = jnp.dot(hn, wu_ref[0], preferred_element_type=F32)
        act = ((hg * jax.nn.sigmoid(hg)) * hu).astype(BF16)
        o_ref[...] = jnp.dot(act, wd_ref[0], preferred_element_type=F32)

    @pl.when(b >= nused_ref[0])
    def _():
        o_ref[...] = jnp.zeros_like(o_ref)


def _moe_ffn(x, g_ffn, blk_eid, n_used, row_tok, w_gate, w_up, w_down, *, blk, layer):
    t, d = x.shape
    n_rows = row_tok.shape[0]
    n_blocks = n_rows // blk
    dff = w_gate.shape[-1]
    first = layer * MOE_N_EXPERTS
    tok3 = row_tok.reshape(n_blocks, 1, blk)
    grid_spec = pltpu.PrefetchScalarGridSpec(
        num_scalar_prefetch=2,
        grid=(n_blocks,),
        in_specs=[
            pl.BlockSpec((1, 1, blk), lambda b, e, u: (b, 0, 0), memory_space=pltpu.SMEM),
            pl.BlockSpec((1, 1, blk), lambda b, e, u: (jnp.minimum(b + 1, n_blocks - 1), 0, 0),
                         memory_space=pltpu.SMEM),
            pl.BlockSpec(memory_space=pl.ANY),
            pl.BlockSpec((1, d), lambda b, e, u: (0, 0)),
            pl.BlockSpec((1, d, dff), lambda b, e, u: (first + e[b], 0, 0)),
            pl.BlockSpec((1, d, dff), lambda b, e, u: (first + e[b], 0, 0)),
            pl.BlockSpec((1, dff, d), lambda b, e, u: (first + e[b], 0, 0)),
        ],
        out_specs=pl.BlockSpec((blk, d), lambda b, e, u: (b, 0)),
        scratch_shapes=[pltpu.VMEM((2, blk, d), F32), pltpu.SemaphoreType.DMA((2,)),
                        pltpu.VMEM((1, d, dff), BF16), pltpu.VMEM((1, d, dff), BF16),
                        pltpu.VMEM((1, dff, d), BF16)],
    )
    return pl.pallas_call(
        functools.partial(_moe_kernel, blk=blk),
        grid_spec=grid_spec,
        out_shape=jax.ShapeDtypeStruct((n_rows, d), F32),
        compiler_params=_cparams(("arbitrary",)),
    )(blk_eid, n_used, tok3, tok3, x, g_ffn.reshape(1, d), w_gate, w_up, w_down)


def _combine_kernel(pos_ref, pos_next_ref, x_ref, rw_ref, y_hbm, g_ref, o_ref, ybuf, sem,
                    *, tc, final_norm):
    i = pl.program_id(0)
    n = pl.num_programs(0)
    slot = i % 2

    @pl.when(i == 0)
    def _():
        _gather_rows(pos_ref, 2 * tc, y_hbm, ybuf.at[0], sem.at[0])

    @pl.when(i + 1 < n)
    def _():
        _gather_rows(pos_next_ref, 2 * tc, y_hbm, ybuf.at[1 - slot], sem.at[1 - slot])

    _wait_rows(2 * tc, y_hbm, ybuf.at[slot], sem.at[slot])

    rw = rw_ref[...]
    y = x_ref[...] + rw[:, 0:1] * ybuf[slot, pl.ds(0, tc), :] + rw[:, 1:2] * ybuf[slot, pl.ds(tc, tc), :]
    if final_norm:
        ms = jnp.mean(y * y, axis=-1, keepdims=True)
        y = (y * lax.rsqrt(ms + RMS_EPS)) * g_ref[...]
    o_ref[...] = y


def _combine(x, route_w, pos, y_sorted, g_final, *, final_norm, tc=256):
    t, d = x.shape
    tc = min(tc, t)
    assert t % tc == 0
    nt = t // tc
    pos3 = pos.reshape(nt, tc, 2).transpose(0, 2, 1).reshape(nt, 1, 2 * tc)
    return pl.pallas_call(
        functools.partial(_combine_kernel, tc=tc, final_norm=final_norm),
        grid=(nt,),
        in_specs=[
            pl.BlockSpec((1, 1, 2 * tc), lambda i: (i, 0, 0), memory_space=pltpu.SMEM),
            pl.BlockSpec((1, 1, 2 * tc), lambda i: (jnp.minimum(i + 1, nt - 1), 0, 0),
                         memory_space=pltpu.SMEM),
            pl.BlockSpec((tc, d), lambda i: (i, 0)),
            pl.BlockSpec((tc, LANES), lambda i: (i, 0)),
            pl.BlockSpec(memory_space=pl.ANY),
            pl.BlockSpec((1, d), lambda i: (0, 0)),
        ],
        out_specs=pl.BlockSpec((tc, d), lambda i: (i, 0)),
        out_shape=jax.ShapeDtypeStruct((t, d), F32),
        scratch_shapes=[pltpu.VMEM((2, 2 * tc, d), F32), pltpu.SemaphoreType.DMA((2,))],
        compiler_params=_cparams(("arbitrary",)),
    )(pos3, pos3, x, route_w, y_sorted, g_final.reshape(1, d))


def _dispatch(route_i, blk):
    t = route_i.shape[0]
    eid = route_i[:, :2].reshape(-1)
    n_assign = eid.shape[0]
    onehot = (eid[:, None] == jnp.arange(MOE_N_EXPERTS, dtype=I32)[None, :]).astype(F32)
    tile = 256
    oh = onehot.reshape(n_assign // tile, tile, MOE_N_EXPERTS)
    within = jnp.einsum('ij,tje->tie', jnp.tril(jnp.ones((tile, tile), F32)), oh)
    tile_tot = within[:, -1, :]
    tile_off = jnp.cumsum(tile_tot, axis=0) - tile_tot
    csum = within + tile_off[:, None, :]
    rank = (jnp.sum(csum * oh, axis=2) - 1.0).reshape(n_assign).astype(I32)
    counts = (tile_off[-1] + tile_tot[-1]).astype(I32)
    padded = ((counts + blk - 1) // blk) * blk
    pend = jnp.cumsum(padded)
    pstart = pend - padded
    dest = (pstart[eid] + rank).astype(I32)
    n_rows = n_assign + MOE_N_EXPERTS * blk
    n_blocks = n_rows // blk
    row_tok = jnp.zeros((n_rows,), I32).at[dest].set(jnp.arange(n_assign, dtype=I32) // 2)
    blk_start = jnp.arange(n_blocks, dtype=I32) * blk
    blk_eid = jnp.minimum(jnp.sum((pend[None, :] <= blk_start[:, None]).astype(I32), axis=1),
                          MOE_N_EXPERTS - 1)
    n_used = (pend[-1:] // blk).astype(I32)
    return row_tok, blk_eid, n_used, dest.reshape(t, 2)


def _router_weights(w_group, w_expert):
    d = w_group.shape[0]
    w = jnp.zeros((d, LANES), F32)
    w = w.at[:, :MOE_GROUPS].set(w_group).at[:, MOE_GROUPS:MOE_GROUPS + MOE_N_EXPERTS].set(w_expert)
    return w.astype(BF16)


def _moe_layer(x1, route_i, route_w, g_ffn, w_gate, w_up, w_down, g_final, *, layer, final_norm,
               blk=256):
    row_tok, blk_eid, n_used, pos = _dispatch(route_i, blk)
    stack = lambda w: w.reshape((-1,) + w.shape[2:])
    y_sorted = _moe_ffn(x1, g_ffn, blk_eid, n_used, row_tok, stack(w_gate), stack(w_up),
                        stack(w_down), blk=blk, layer=layer)
    return _combine(x1, route_w, pos, y_sorted, g_final, final_norm=final_norm)


def kernel(x, norm_mix, norm_ffn, gla_w_in, gla_w_gate_up, gla_b_gate, gla_norm, gla_w_out,
           diff_w_in, diff_lambda_q1, diff_lambda_k1, diff_lambda_q2, diff_lambda_k2,
           diff_subln, diff_w_out, moe_w_group, moe_w_expert, moe_w_gate, moe_w_up,
           moe_w_down, final_norm):
    batch, seq, d = x.shape
    t = batch * seq
    xf = x.reshape(t, d)

    dk_total = d // 2
    n_main = 2 * dk_total + 2 * d
    w_in = gla_w_in[0]
    gla_dk = dk_total // GLA_HEADS
    colscale = jnp.ones((n_main,), F32).at[:dk_total].set(gla_dk ** -0.5)
    rank = w_in.shape[1] - n_main
    w_gz = jnp.zeros((d, LANES), F32).at[:, :rank].set(w_in[:, n_main:]).astype(BF16)
    proj, gz = _norm_matmul(xf, norm_mix[0], w_in[:, :n_main].astype(BF16), colscale, w_gz)
    o = _gla_scan(proj, gz, gla_w_gate_up[0], gla_b_gate[0], gla_norm[0], batch=batch, seq=seq)
    x1, ri, rw = _proj_router(o, gla_w_out[0].astype(BF16), xf, norm_ffn[0],
                              _router_weights(moe_w_group[0], moe_w_expert[0]))
    x2 = _moe_layer(x1, ri, rw, norm_ffn[0], moe_w_gate, moe_w_up, moe_w_down,
                    final_norm, layer=0, final_norm=False)

    hd = d // DIFF_HEADS // 2
    colscale = jnp.ones((3 * d,), F32).at[:d].set(hd ** -0.5 * LOG2E)
    qkv = _norm_matmul(x2, norm_mix[1], diff_w_in[0].astype(BF16), colscale)
    o = _diff_attention(qkv, diff_lambda_q1[0], diff_lambda_k1[0], diff_lambda_q2[0],
                        diff_lambda_k2[0], diff_subln[0], batch=batch, seq=seq, layer_idx=1)
    x3, ri, rw = _proj_router(o, diff_w_out[0].astype(BF16), x2, norm_ffn[1],
                              _router_weights(moe_w_group[1], moe_w_expert[1]))
    out = _moe_layer(x3, ri, rw, norm_ffn[1], moe_w_gate, moe_w_up, moe_w_down,
                     final_norm, layer=1, final_norm=True)
    return out.reshape(batch, seq, d)
```

```python
import functools
import math

import jax
import jax.numpy as jnp
from jax import lax
from jax.experimental import pallas as pl
from jax.experimental.pallas import tpu as pltpu

F32 = jnp.float32
BF16 = jnp.bfloat16
I32 = jnp.int32

RMS_EPS = 1e-6
CHUNK = 64
GLA_BLOCK = 256
GLA_HEADS = 4
GLA_GATE_TAU = 16.0
DIFF_HEADS = 8
MOE_GROUPS = 4
MOE_EXPERTS_PER_GROUP = 8
MOE_N_EXPERTS = MOE_GROUPS * MOE_EXPERTS_PER_GROUP

LANES = 128
SLAB = 256
VMEM_LIMIT = 56 * 1024 * 1024
LOG2E = 1.4426950408889634
NEG_BIG = -1e30
EXP_ZERO = 160.0
NORM_SLACK = 1.0 + 2.0 ** -10
SQ_SLACK = 1.0 + 2.0 ** -7

HIGHEST = lax.Precision.HIGHEST


def _cparams(sem):
    return pltpu.CompilerParams(dimension_semantics=sem, vmem_limit_bytes=VMEM_LIMIT)


def _norm_matmul_kernel(x_ref, g_ref, w_ref, cs_ref, *rest, n_slab, has_aux):
    if has_aux:
        w2_ref, o_ref, o2_ref, hn_ref = rest
    else:
        o_ref, hn_ref = rest

    @pl.when(pl.program_id(1) == 0)
    def _():
        x = x_ref[...]
        ms = jnp.mean(x * x, axis=-1, keepdims=True)
        hn = ((x * lax.rsqrt(ms + RMS_EPS)) * g_ref[...]).astype(BF16)
        hn_ref[...] = hn
        if has_aux:
            o2_ref[...] = jnp.dot(hn, w2_ref[...], preferred_element_type=F32)

    acc = jnp.dot(hn_ref[...], w_ref[...], preferred_element_type=F32) * cs_ref[...]
    for s in range(n_slab):
        o_ref[s] = acc[:, s * SLAB:(s + 1) * SLAB].astype(BF16)


def _norm_matmul(x, g, w, colscale, w_aux=None, *, tm=1024, tn=1024):
    t, d = x.shape
    n = w.shape[1]
    tm = min(tm, t)
    assert t % tm == 0 and n % tn == 0 and tn % SLAB == 0
    n_slab = tn // SLAB
    has_aux = w_aux is not None
    in_specs = [
        pl.BlockSpec((tm, d), lambda i, j: (i, 0)),
        pl.BlockSpec((1, d), lambda i, j: (0, 0)),
        pl.BlockSpec((d, tn), lambda i, j: (0, j)),
        pl.BlockSpec((1, tn), lambda i, j: (0, j)),
    ]
    out_shape = [jax.ShapeDtypeStruct((n // SLAB, t, SLAB), BF16)]
    out_specs = [pl.BlockSpec((n_slab, tm, SLAB), lambda i, j: (j, i, 0))]
    args = [x, g.reshape(1, d), w, colscale.reshape(1, n)]
    if has_aux:
        in_specs.append(pl.BlockSpec((d, LANES), lambda i, j: (0, 0)))
        out_shape.append(jax.ShapeDtypeStruct((t, LANES), F32))
        out_specs.append(pl.BlockSpec((tm, LANES), lambda i, j: (i, 0)))
        args.append(w_aux)
    res = pl.pallas_call(
        functools.partial(_norm_matmul_kernel, n_slab=n_slab, has_aux=has_aux),
        grid=(t // tm, n // tn),
        in_specs=in_specs,
        out_specs=out_specs,
        out_shape=out_shape,
        scratch_shapes=[pltpu.VMEM((tm, d), BF16)],
        compiler_params=_cparams(("parallel", "arbitrary")),
    )(*args)
    return res if has_aux else res[0]


def _gla_kernel(q_ref, k_ref, v_ref, r_ref, gz_ref, wg_ref, bg_ref, ng_ref, o_ref,
                state_ref, la_ref, ob_ref, *, n_blocks):
    @pl.when(pl.program_id(2) == 0)
    def _():
        state_ref[...] = jnp.zeros_like(state_ref)

    def split3(x):
        hi = x.astype(BF16)
        r1 = x - hi.astype(F32)
        mid = r1.astype(BF16)
        return hi, mid, (r1 - mid.astype(F32)).astype(BF16)

    def dot32(a, b):
        return jnp.dot(a, b, preferred_element_type=F32)

    gh, gm, _ = split3(gz_ref[...])
    wh, wm, _ = split3(wg_ref[...])
    z = dot32(gh, wh) + dot32(gm, wh) + dot32(gh, wm) + bg_ref[...]
    la_ref[...] = -(jnp.maximum(-z, 0.0) + jnp.log1p(jnp.exp(-jnp.abs(z)))) * (1.0 / GLA_GATE_TAU)

    nb = GLA_BLOCK
    nc = nb // CHUNK
    row = lax.broadcasted_iota(I32, (nb, nb), 0)
    col = lax.broadcasted_iota(I32, (nb, nb), 1)
    same_chunk = (row // CHUNK) == (col // CHUNK)
    later = (same_chunk & (col > row)).astype(BF16)
    visible = (row // CHUNK) >= (col // CHUNK)
    dk = state_ref.shape[0]
    dv = state_ref.shape[1]
    nt = (((1,), (1,)), ((), ()))

    def per_chunk_rows(vals):
        return jnp.concatenate([jnp.broadcast_to(x, (CHUNK, dk)) for x in vals], axis=0)

    def body(i, carry):
        sl = pl.ds(pl.multiple_of(i * nb, nb), nb)
        la = la_ref[sl, :]
        suf = sum(dot32(later, t) for t in split3(la))
        tot = [suf[c * CHUNK:c * CHUNK + 1] + la[c * CHUNK:c * CHUNK + 1] for c in range(nc)]
        a = [tot[0]]
        for c in range(1, nc):
            a.append(a[-1] + tot[c])
        qf = q_ref[0, sl, :].astype(F32)
        k_dec = k_ref[0, sl, :].astype(F32) * jnp.exp(suf)
        v = jnp.concatenate([v_ref[0, sl, :], v_ref[1, sl, :]], axis=-1)
        s0 = state_ref[...]

        p = None
        zeros = jnp.zeros((CHUNK, dk), BF16)
        k_dec16 = k_dec.astype(BF16)
        for i_c in range(nc):
            q_i = (qf * per_chunk_rows([jnp.exp(jnp.minimum(a[c] - a[i_c], 0.0))
                                        for c in range(nc)])).astype(BF16)
            k_i = jnp.concatenate([k_dec16[c * CHUNK:(c + 1) * CHUNK] if c == i_c else zeros
                                   for c in range(nc)], axis=0)
            term = lax.dot_general(q_i, k_i, nt, preferred_element_type=F32)
            p = term if p is None else p + term
        p = jnp.where(visible, p, 0.0).astype(BF16)
        q_in = (qf * per_chunk_rows([jnp.exp(a[c]) for c in range(nc)])).astype(BF16)
        ob_ref[sl, :] = dot32(q_in, s0.astype(BF16)) + dot32(p, v)

        k_out = (k_dec * per_chunk_rows([jnp.exp(a[nc - 1] - a[c]) for c in range(nc)])).astype(BF16)
        upd = lax.dot_general(k_out, v, (((0,), (0,)), ((), ())), preferred_element_type=F32)
        decay = jnp.exp(jnp.broadcast_to(a[nc - 1], (LANES, dk)).T)
        state_ref[...] = jnp.concatenate([decay] * (dv // LANES), axis=-1) * s0 + upd
        return carry

    lax.fori_loop(0, n_blocks, body, 0)

    o = ob_ref[...]
    ms = jnp.mean(o * o, axis=-1, keepdims=True)
    on = (o * lax.rsqrt(ms + RMS_EPS)) * ng_ref[...]
    r = jnp.concatenate([r_ref[0], r_ref[1]], axis=-1).astype(F32)
    o_ref[...] = (on * (r * jax.nn.sigmoid(r))).astype(BF16)


def _gla_scan(proj, gz, w_gate_up, b_gate, norm_g, *, batch, seq, lc=1024):
    t = batch * seq
    lc = min(lc, seq)
    assert seq % lc == 0 and lc % GLA_BLOCK == 0 and GLA_BLOCK % CHUNK == 0
    ncb = seq // lc
    dk = SLAB
    dv = 2 * SLAB
    d_out = GLA_HEADS * dv
    rank = w_gate_up.shape[0]
    wg = jnp.zeros((LANES, GLA_HEADS * dk), F32).at[:rank].set(w_gate_up)

    def tok(b, h, c):
        return b * ncb + c

    return pl.pallas_call(
        functools.partial(_gla_kernel, n_blocks=lc // GLA_BLOCK),
        grid=(batch, GLA_HEADS, ncb),
        in_specs=[
            pl.BlockSpec((1, lc, SLAB), lambda b, h, c: (h, tok(b, h, c), 0)),
            pl.BlockSpec((1, lc, SLAB), lambda b, h, c: (GLA_HEADS + h, tok(b, h, c), 0)),
            pl.BlockSpec((2, lc, SLAB), lambda b, h, c: (GLA_HEADS + h, tok(b, h, c), 0)),
            pl.BlockSpec((2, lc, SLAB), lambda b, h, c: (2 * GLA_HEADS + h, tok(b, h, c), 0)),
            pl.BlockSpec((lc, LANES), lambda b, h, c: (tok(b, h, c), 0)),
            pl.BlockSpec((LANES, dk), lambda b, h, c: (0, h)),
            pl.BlockSpec((1, dk), lambda b, h, c: (0, h)),
            pl.BlockSpec((1, dv), lambda b, h, c: (0, 0)),
        ],
        out_specs=pl.BlockSpec((lc, dv), lambda b, h, c: (tok(b, h, c), h)),
        out_shape=jax.ShapeDtypeStruct((t, d_out), BF16),
        scratch_shapes=[pltpu.VMEM((dk, dv), F32), pltpu.VMEM((lc, dk), F32),
                        pltpu.VMEM((lc, dv), F32)],
        compiler_params=_cparams(("parallel", "parallel", "arbitrary")),
    )(proj, proj, proj, proj, gz, wg, b_gate.reshape(1, -1), norm_g.reshape(1, -1))


def _diff_attn_kernel(slope_ref, invct_ref, q_ref, k_ref, v_ref, qfeat_ref, kfeat_ref, lq1_ref, lk1_ref,
                      lq2_ref, lk2_ref, sg_ref, o_ref, vt_ref, dbias_ref, r0_ref, kmax_ref, m_ref, l_ref, acc_ref,
                      *, tq, lambda_init):
    h = pl.program_id(1)
    qi = pl.program_id(2)
    hd = SLAB // 2
    c = slope_ref[h] * LOG2E
    nt = (((1,), (1,)), ((), ()))

    ones16 = jnp.ones((16, hd), BF16)

    def sq_norms(x):
        xf = x.astype(F32)
        sq = (xf * xf * SQ_SLACK).astype(BF16)
        return lax.dot_general(ones16, sq, nt, preferred_element_type=F32)[0:1]

    @pl.when(qi == 0)
    def _():
        def per_block(i, carry):
            sl = pl.ds(pl.multiple_of(i * tq, tq), tq)
            vt_ref[:, sl] = v_ref[0, sl, :].T
            kb = k_ref[0, sl, :]
            return tuple(jnp.maximum(carry[u], sq_norms(kb[:, u * hd:(u + 1) * hd]))
                         for u in range(2))
        zero = jnp.zeros((1, tq), F32)
        kn2 = lax.fori_loop(0, v_ref.shape[1] // tq, per_block, (zero, zero))
        for u in range(2):
            kmax_ref[u] = jnp.broadcast_to(jnp.sqrt(jnp.max(kn2[u], axis=1, keepdims=True)),
                                           (8, LANES))
        ki = lax.broadcasted_iota(I32, (tq, tq), 0)
        qj = lax.broadcasted_iota(I32, (tq, tq), 1)
        bias = c * (qj - jnp.abs(qj - ki)).astype(F32)
        dbias_ref[...] = jnp.where((qj // CHUNK) >= (ki // CHUNK), bias, -jnp.inf)

    q = q_ref[0]
    start = pl.multiple_of(qi * tq, tq)

    k = k_ref[0, pl.ds(start, tq), :]
    vt = vt_ref[:, pl.ds(start, tq)]
    for u in range(2):
        r = lax.dot_general(k[:, u * hd:(u + 1) * hd], q[:, u * hd:(u + 1) * hd], nt,
                            preferred_element_type=F32) + dbias_ref[...]
        m_new = jnp.max(r, axis=0, keepdims=True)
        p = jnp.exp2(r - m_new)
        l_ref[u] = jnp.sum(p, axis=0, keepdims=True)
        acc_ref[u] = jnp.dot(vt, p.astype(BF16), preferred_element_type=F32)
        m_ref[u] = m_new

    qfeat = jnp.broadcast_to(qfeat_ref[0], (tq, LANES))
    kfeat = kfeat_ref[...]
    qa = [jnp.concatenate([q[:, u * hd:(u + 1) * hd], qfeat], axis=-1) for u in range(2)]

    def scores(u, blk):
        st = pl.multiple_of(blk * tq, tq)
        ka = jnp.concatenate([k_ref[0, pl.ds(st, tq), u * hd:(u + 1) * hd], kfeat], axis=-1)
        return lax.dot_general(ka, qa[u], nt, preferred_element_type=F32)

    def accumulate(u, r, vt, off):
        m_prev = m_ref[u]
        m_new = jnp.maximum(m_prev, jnp.max(r, axis=0, keepdims=True) - off)
        alpha = jnp.exp2(m_prev - m_new)
        p = jnp.exp2(r - (m_new + off))
        l_ref[u] = alpha * l_ref[u] + jnp.sum(p, axis=0, keepdims=True)
        acc_ref[u] = alpha * acc_ref[u] + jnp.dot(vt, p.astype(BF16),
                                                  preferred_element_type=F32)
        m_ref[u] = m_new

    r0_ref[...] = scores(0, jnp.maximum(qi - 1, 0))

    def past_block(dd, carry):
        blk = qi - dd
        vt = vt_ref[:, pl.ds(pl.multiple_of(blk * tq, tq), tq)]
        off = c * (dd * tq).astype(F32)
        r1 = scores(1, blk)
        accumulate(0, r0_ref[...], vt, off)
        r0_ref[...] = scores(0, jnp.maximum(blk - 1, 0))
        accumulate(1, r1, vt, off)
        return carry

    gap = None
    for u in range(2):
        bound = (jnp.sqrt(sq_norms(q[:, u * hd:(u + 1) * hd])) * kmax_ref[u][0:1, 0:1]
                 * NORM_SLACK - m_ref[u])
        g = jnp.max(bound, axis=1, keepdims=True)
        gap = g if gap is None else jnp.maximum(gap, g)
    n_need = jnp.ceil((gap + EXP_ZERO) * invct_ref[h])
    n_need = jnp.clip(n_need, 0.0, float(v_ref.shape[1] // tq)).astype(I32)[0, 0]
    lax.fori_loop(1, jnp.minimum(qi, n_need) + 1, past_block, 0)

    lam = (jnp.exp(jnp.sum(lq1_ref[...] * lk1_ref[...], axis=-1, keepdims=True))
           - jnp.exp(jnp.sum(lq2_ref[...] * lk2_ref[...], axis=-1, keepdims=True))
           + lambda_init)
    o = acc_ref[0] * (1.0 / l_ref[0]) - lam * (acc_ref[1] * (1.0 / l_ref[1]))
    o = o.T
    ms = jnp.mean(o * o, axis=-1, keepdims=True)
    o = (o * lax.rsqrt(ms + RMS_EPS)) * sg_ref[...]
    o_ref[...] = (o * (1.0 - lambda_init)).astype(BF16)


def _alibi_features(slopes, tq):
    c = slopes * LOG2E
    c_hi = c.astype(BF16).astype(F32)
    c_lo = c - c_hi
    qf = jnp.zeros((slopes.shape[0], 1, LANES), F32)
    qf = qf.at[:, 0, 0].set(LANES * c_hi).at[:, 0, 1].set(c_hi)
    qf = qf.at[:, 0, 2].set(LANES * c_lo).at[:, 0, 3].set(c_lo)
    idx = jnp.arange(tq, dtype=I32)
    hi = (idx // LANES).astype(F32)
    lo = (idx % LANES).astype(F32)
    kf = jnp.zeros((tq, LANES), F32).at[:, 0].set(hi).at[:, 1].set(lo).at[:, 2].set(hi).at[:, 3].set(lo)
    return qf.astype(BF16), kf.astype(BF16)


def _diff_attention(qkv, lq1, lk1, lq2, lk2, subln, *, batch, seq, layer_idx, tq=512):
    t = batch * seq
    tq = min(tq, seq)
    assert seq % tq == 0 and tq % CHUNK == 0
    nq = seq // tq
    lambda_init = 0.8 - 0.6 * math.exp(-0.3 * layer_idx)
    slopes = 2.0 ** (-8.0 * jnp.arange(1, DIFF_HEADS + 1, dtype=F32) / DIFF_HEADS)
    vec = lambda a: a.reshape(1, -1).astype(F32)
    qfeat, kfeat = _alibi_features(slopes, tq)
    grid_spec = pltpu.PrefetchScalarGridSpec(
        num_scalar_prefetch=2,
        grid=(batch, DIFF_HEADS, nq),
        in_specs=[
            pl.BlockSpec((1, tq, SLAB), lambda b, h, i, s, n: (h, b * nq + i, 0)),
            pl.BlockSpec((1, seq, SLAB), lambda b, h, i, s, n: (DIFF_HEADS + h, b, 0)),
            pl.BlockSpec((1, seq, SLAB), lambda b, h, i, s, n: (2 * DIFF_HEADS + h, b, 0)),
            pl.BlockSpec((1, 1, LANES), lambda b, h, i, s, n: (h, 0, 0)),
            pl.BlockSpec((tq, LANES), lambda b, h, i, s, n: (0, 0)),
            pl.BlockSpec((1, SLAB // 2), lambda b, h, i, s, n: (0, 0)),
            pl.BlockSpec((1, SLAB // 2), lambda b, h, i, s, n: (0, 0)),
            pl.BlockSpec((1, SLAB // 2), lambda b, h, i, s, n: (0, 0)),
            pl.BlockSpec((1, SLAB // 2), lambda b, h, i, s, n: (0, 0)),
            pl.BlockSpec((1, SLAB), lambda b, h, i, s, n: (0, 0)),
        ],
        out_specs=pl.BlockSpec((tq, SLAB), lambda b, h, i, s, n: (b * nq + i, h)),
        scratch_shapes=[pltpu.VMEM((SLAB, seq), BF16), pltpu.VMEM((tq, tq), F32),
                        pltpu.VMEM((tq, tq), F32), pltpu.VMEM((2, 8, LANES), F32),
                        pltpu.VMEM((2, 1, tq), F32), pltpu.VMEM((2, 1, tq), F32),
                        pltpu.VMEM((2, SLAB, tq), F32)],
    )
    return pl.pallas_call(
        functools.partial(_diff_attn_kernel, tq=tq, lambda_init=lambda_init),
        grid_spec=grid_spec,
        out_shape=jax.ShapeDtypeStruct((t, DIFF_HEADS * SLAB), BF16),
        compiler_params=_cparams(("parallel", "parallel", "arbitrary")),
    )(slopes, 1.0 / (slopes * (LOG2E * tq)), qkv, qkv, qkv, qfeat, kfeat, vec(lq1), vec(lk1), vec(lq2), vec(lk2), vec(subln))


def _proj_router_kernel(o_ref, w_ref, x_ref, g_ref, wr_ref, x1_ref, ri_ref, rw_ref):
    x1 = x_ref[...] + jnp.dot(o_ref[...], w_ref[...], preferred_element_type=F32)
    x1_ref[...] = x1
    ms = jnp.mean(x1 * x1, axis=-1, keepdims=True)
    hn = ((x1 * lax.rsqrt(ms + RMS_EPS)) * g_ref[...]).astype(BF16)
    logits = jnp.dot(hn, wr_ref[...], preferred_element_type=F32)
    lane = lax.broadcasted_iota(I32, logits.shape, 1)
    neg = -jnp.inf

    def first_max(vals):
        m = jnp.max(vals, axis=-1, keepdims=True)
        idx = jnp.min(jnp.where(vals == m, lane, LANES), axis=-1, keepdims=True)
        return m, idx

    gmask = lane < MOE_GROUPS
    gmax, gidx = first_max(jnp.where(gmask, logits, neg))
    gden = jnp.sum(jnp.where(gmask, jnp.exp(logits - gmax), 0.0), axis=-1, keepdims=True)
    g_w = 1.0 / gden
    lo = MOE_GROUPS + MOE_EXPERTS_PER_GROUP * gidx
    emask = (lane >= lo) & (lane < lo + MOE_EXPERTS_PER_GROUP)
    elog = jnp.where(emask, logits, neg)
    m1, i1 = first_max(elog)
    m2, i2 = first_max(jnp.where(lane == i1, neg, elog))
    tt = jnp.exp(m2 - m1)
    w1 = g_w / (1.0 + tt)
    w2 = w1 * tt
    ri_ref[...] = jnp.where(lane == 0, i1 - MOE_GROUPS, jnp.where(lane == 1, i2 - MOE_GROUPS, 0))
    rw_ref[...] = jnp.where(lane == 0, w1, jnp.where(lane == 1, w2, 0.0))


def _proj_router(o, w_out, x, g_ffn, w_router, *, tm=512):
    t, d = x.shape
    tm = min(tm, t)
    assert t % tm == 0
    return pl.pallas_call(
        _proj_router_kernel,
        grid=(t // tm,),
        in_specs=[
            pl.BlockSpec((tm, d), lambda i: (i, 0)),
            pl.BlockSpec((d, d), lambda i: (0, 0)),
            pl.BlockSpec((tm, d), lambda i: (i, 0)),
            pl.BlockSpec((1, d), lambda i: (0, 0)),
            pl.BlockSpec((d, LANES), lambda i: (0, 0)),
        ],
        out_specs=[
            pl.BlockSpec((tm, d), lambda i: (i, 0)),
            pl.BlockSpec((tm, LANES), lambda i: (i, 0)),
            pl.BlockSpec((tm, LANES), lambda i: (i, 0)),
        ],
        out_shape=[jax.ShapeDtypeStruct((t, d), F32), jax.ShapeDtypeStruct((t, LANES), I32),
                   jax.ShapeDtypeStruct((t, LANES), F32)],
        compiler_params=_cparams(("parallel",)),
    )(o, w_out, x, g_ffn.reshape(1, d), w_router)


def _gather_rows(idx_ref, n, src_hbm, dst_ref, sem):
    def body(r, carry):
        t = idx_ref[0, 0, r]
        pltpu.make_async_copy(src_hbm.at[pl.ds(t, 1), :], dst_ref.at[pl.ds(r, 1), :], sem).start()
        return carry
    lax.fori_loop(0, n, body, 0, unroll=8)


def _wait_rows(n, src_hbm, dst_ref, sem):
    pltpu.make_async_copy(src_hbm.at[pl.ds(0, n), :], dst_ref, sem).wait()


def _moe_kernel(eid_ref, nused_ref, tok_ref, tok_next_ref, x_hbm, g_ref, wg32_ref, wu32_ref,
                wd32_ref, o_ref, xbuf, sem, wg_ref, wu_ref, wd_ref, *, blk):
    b = pl.program_id(0)
    nb = pl.num_programs(0)
    slot = b % 2

    @pl.when((b == 0) | (eid_ref[b] != eid_ref[jnp.maximum(b - 1, 0)]))
    def _():
        wg_ref[0] = wg32_ref[0].astype(BF16)
        wu_ref[0] = wu32_ref[0].astype(BF16)
        wd_ref[0] = wd32_ref[0].astype(BF16)

    @pl.when(b == 0)
    def _():
        _gather_rows(tok_ref, blk, x_hbm, xbuf.at[0], sem.at[0])

    @pl.when(b + 1 < nb)
    def _():
        _gather_rows(tok_next_ref, blk, x_hbm, xbuf.at[1 - slot], sem.at[1 - slot])

    _wait_rows(blk, x_hbm, xbuf.at[slot], sem.at[slot])

    @pl.when(b < nused_ref[0])
    def _():
        x = xbuf[slot]
        ms = jnp.mean(x * x, axis=-1, keepdims=True)
        hn = ((x * lax.rsqrt(ms + RMS_EPS)) * g_ref[...]).astype(BF16)
        hg = jnp.dot(hn, wg_ref[0], preferred_element_type=F32)
        hu = jnp.dot(hn, wu_ref[0], preferred_element_type=F32)
        act = ((hg * jax.nn.sigmoid(hg)) * hu).astype(BF16)
        o_ref[...] = jnp.dot(act, wd_ref[0], preferred_element_type=F32)

    @pl.when(b >= nused_ref[0])
    def _():
        o_ref[...] = jnp.zeros_like(o_ref)


def _moe_ffn(x, g_ffn, blk_eid, n_used, row_tok, w_gate, w_up, w_down, *, blk, layer):
    t, d = x.shape
    n_rows = row_tok.shape[0]
    n_blocks = n_rows // blk
    dff = w_gate.shape[-1]
    first = layer * MOE_N_EXPERTS
    tok3 = row_tok.reshape(n_blocks, 1, blk)
    grid_spec = pltpu.PrefetchScalarGridSpec(
        num_scalar_prefetch=2,
        grid=(n_blocks,),
        in_specs=[
            pl.BlockSpec((1, 1, blk), lambda b, e, u: (b, 0, 0), memory_space=pltpu.SMEM),
            pl.BlockSpec((1, 1, blk), lambda b, e, u: (jnp.minimum(b + 1, n_blocks - 1), 0, 0),
                         memory_space=pltpu.SMEM),
            pl.BlockSpec(memory_space=pl.ANY),
            pl.BlockSpec((1, d), lambda b, e, u: (0, 0)),
            pl.BlockSpec((1, d, dff), lambda b, e, u: (first + e[b], 0, 0)),
            pl.BlockSpec((1, d, dff), lambda b, e, u: (first + e[b], 0, 0)),
            pl.BlockSpec((1, dff, d), lambda b, e, u: (first + e[b], 0, 0)),
        ],
        out_specs=pl.BlockSpec((blk, d), lambda b, e, u: (b, 0)),
        scratch_shapes=[pltpu.VMEM((2, blk, d), F32), pltpu.SemaphoreType.DMA((2,)),
                        pltpu.VMEM((1, d, dff), BF16), pltpu.VMEM((1, d, dff), BF16),
                        pltpu.VMEM((1, dff, d), BF16)],
    )
    return pl.pallas_call(
        functools.partial(_moe_kernel, blk=blk),
        grid_spec=grid_spec,
        out_shape=jax.ShapeDtypeStruct((n_rows, d), F32),
        compiler_params=_cparams(("arbitrary",)),
    )(blk_eid, n_used, tok3, tok3, x, g_ffn.reshape(1, d), w_gate, w_up, w_down)


def _combine_kernel(pos_ref, pos_next_ref, x_ref, rw_ref, y_hbm, g_ref, o_ref, ybuf, sem,
                    *, tc, final_norm):
    i = pl.program_id(0)
    n = pl.num_programs(0)
    slot = i % 2

    @pl.when(i == 0)
    def _():
        _gather_rows(pos_ref, 2 * tc, y_hbm, ybuf.at[0], sem.at[0])

    @pl.when(i + 1 < n)
    def _():
        _gather_rows(pos_next_ref, 2 * tc, y_hbm, ybuf.at[1 - slot], sem.at[1 - slot])

    _wait_rows(2 * tc, y_hbm, ybuf.at[slot], sem.at[slot])

    rw = rw_ref[...]
    y = x_ref[...] + rw[:, 0:1] * ybuf[slot, pl.ds(0, tc), :] + rw[:, 1:2] * ybuf[slot, pl.ds(tc, tc), :]
    if final_norm:
        ms = jnp.mean(y * y, axis=-1, keepdims=True)
        y = (y * lax.rsqrt(ms + RMS_EPS)) * g_ref[...]
    o_ref[...] = y


def _combine(x, route_w, pos, y_sorted, g_final, *, final_norm, tc=256):
    t, d = x.shape
    tc = min(tc, t)
    assert t % tc == 0
    nt = t // tc
    pos3 = pos.reshape(nt, tc, 2).transpose(0, 2, 1).reshape(nt, 1, 2 * tc)
    return pl.pallas_call(
        functools.partial(_combine_kernel, tc=tc, final_norm=final_norm),
        grid=(nt,),
        in_specs=[
            pl.BlockSpec((1, 1, 2 * tc), lambda i: (i, 0, 0), memory_space=pltpu.SMEM),
            pl.BlockSpec((1, 1, 2 * tc), lambda i: (jnp.minimum(i + 1, nt - 1), 0, 0),
                         memory_space=pltpu.SMEM),
            pl.BlockSpec((tc, d), lambda i: (i, 0)),
            pl.BlockSpec((tc, LANES), lambda i: (i, 0)),
            pl.BlockSpec(memory_space=pl.ANY),
            pl.BlockSpec((1, d), lambda i: (0, 0)),
        ],
        out_specs=pl.BlockSpec((tc, d), lambda i: (i, 0)),
        out_shape=jax.ShapeDtypeStruct((t, d), F32),
        scratch_shapes=[pltpu.VMEM((2, 2 * tc, d), F32), pltpu.SemaphoreType.DMA((2,))],
        compiler_params=_cparams(("arbitrary",)),
    )(pos3, pos3, x, route_w, y_sorted, g_final.reshape(1, d))


def _dispatch(route_i, blk):
    t = route_i.shape[0]
    eid = route_i[:, :2].reshape(-1)
    n_assign = eid.shape[0]
    onehot = (eid[:, None] == jnp.arange(MOE_N_EXPERTS, dtype=I32)[None, :]).astype(F32)
    tile = 256
    oh = onehot.reshape(n_assign // tile, tile, MOE_N_EXPERTS)
    within = jnp.einsum('ij,tje->tie', jnp.tril(jnp.ones((tile, tile), F32)), oh)
    tile_tot = within[:, -1, :]
    tile_off = jnp.cumsum(tile_tot, axis=0) - tile_tot
    csum = within + tile_off[:, None, :]
    rank = (jnp.sum(csum * oh, axis=2) - 1.0).reshape(n_assign).astype(I32)
    counts = (tile_off[-1] + tile_tot[-1]).astype(I32)
    padded = ((counts + blk - 1) // blk) * blk
    pend = jnp.cumsum(padded)
    pstart = pend - padded
    dest = (pstart[eid] + rank).astype(I32)
    n_rows = n_assign + MOE_N_EXPERTS * blk
    n_blocks = n_rows // blk
    row_tok = jnp.zeros((n_rows,), I32).at[dest].set(jnp.arange(n_assign, dtype=I32) // 2)
    blk_start = jnp.arange(n_blocks, dtype=I32) * blk
    blk_eid = jnp.minimum(jnp.sum((pend[None, :] <= blk_start[:, None]).astype(I32), axis=1),
                          MOE_N_EXPERTS - 1)
    n_used = (pend[-1:] // blk).astype(I32)
    return row_tok, blk_eid, n_used, dest.reshape(t, 2)


def _router_weights(w_group, w_expert):
    d = w_group.shape[0]
    w = jnp.zeros((d, LANES), F32)
    w = w.at[:, :MOE_GROUPS].set(w_group).at[:, MOE_GROUPS:MOE_GROUPS + MOE_N_EXPERTS].set(w_expert)
    return w.astype(BF16)


def _moe_layer(x1, route_i, route_w, g_ffn, w_gate, w_up, w_down, g_final, *, layer, final_norm,
               blk=256):
    row_tok, blk_eid, n_used, pos = _dispatch(route_i, blk)
    stack = lambda w: w.reshape((-1,) + w.shape[2:])
    y_sorted = _moe_ffn(x1, g_ffn, blk_eid, n_used, row_tok, stack(w_gate), stack(w_up),
                        stack(w_down), blk=blk, layer=layer)
    return _combine(x1, route_w, pos, y_sorted, g_final, final_norm=final_norm)


def kernel(x, norm_mix, norm_ffn, gla_w_in, gla_w_gate_up, gla_b_gate, gla_norm, gla_w_out,
           diff_w_in, diff_lambda_q1, diff_lambda_k1, diff_lambda_q2, diff_lambda_k2,
           diff_subln, diff_w_out, moe_w_group, moe_w_expert, moe_w_gate, moe_w_up,
           moe_w_down, final_norm):
    batch, seq, d = x.shape
    t = batch * seq
    xf = x.reshape(t, d)

    dk_total = d // 2
    n_main = 2 * dk_total + 2 * d
    w_in = gla_w_in[0]
    gla_dk = dk_total // GLA_HEADS
    colscale = jnp.ones((n_main,), F32).at[:dk_total].set(gla_dk ** -0.5)
    rank = w_in.shape[1] - n_main
    w_gz = jnp.zeros((d, LANES), F32).at[:, :rank].set(w_in[:, n_main:]).astype(BF16)
    proj, gz = _norm_matmul(xf, norm_mix[0], w_in[:, :n_main].astype(BF16), colscale, w_gz)
    o = _gla_scan(proj, gz, gla_w_gate_up[0], gla_b_gate[0], gla_norm[0], batch=batch, seq=seq)
    x1, ri, rw = _proj_router(o, gla_w_out[0].astype(BF16), xf, norm_ffn[0],
                              _router_weights(moe_w_group[0], moe_w_expert[0]))
    x2 = _moe_layer(x1, ri, rw, norm_ffn[0], moe_w_gate, moe_w_up, moe_w_down,
                    final_norm, layer=0, final_norm=False)

    hd = d // DIFF_HEADS // 2
    colscale = jnp.ones((3 * d,), F32).at[:d].set(hd ** -0.5 * LOG2E)
    qkv = _norm_matmul(x2, norm_mix[1], diff_w_in[0].astype(BF16), colscale)
    o = _diff_attention(qkv, diff_lambda_q1[0], diff_lambda_k1[0], diff_lambda_q2[0],
                        diff_lambda_k2[0], diff_subln[0], batch=batch, seq=seq, layer_idx=1)
    x3, ri, rw = _proj_router(o, diff_w_out[0].astype(BF16), x2, norm_ffn[1],
                              _router_weights(moe_w_group[1], moe_w_expert[1]))
    out = _moe_layer(x3, ri, rw, norm_ffn[1], moe_w_gate, moe_w_up, moe_w_down,
                     final_norm, layer=1, final_norm=True)
    return out.reshape(batch, seq, d)
```

```python
import functools
import math

import jax
import jax.numpy as jnp
from jax import lax
from jax.experimental import pallas as pl
from jax.experimental.pallas import tpu as pltpu

F32 = jnp.float32
BF16 = jnp.bfloat16
I32 = jnp.int32

RMS_EPS = 1e-6
CHUNK = 64
GLA_BLOCK = 256
GLA_HEADS = 4
GLA_GATE_TAU = 16.0
DIFF_HEADS = 8
MOE_GROUPS = 4
MOE_EXPERTS_PER_GROUP = 8
MOE_N_EXPERTS = MOE_GROUPS * MOE_EXPERTS_PER_GROUP

LANES = 128
SLAB = 256
VMEM_LIMIT = 56 * 1024 * 1024
LOG2E = 1.4426950408889634
NEG_BIG = -1e30
EXP_ZERO = 160.0
NORM_SLACK = 1.0 + 2.0 ** -10
SQ_SLACK = 1.0 + 2.0 ** -7

HIGHEST = lax.Precision.HIGHEST


def _cparams(sem):
    return pltpu.CompilerParams(dimension_semantics=sem, vmem_limit_bytes=VMEM_LIMIT)


def _norm_matmul_kernel(x_ref, g_ref, w_ref, cs_ref, *rest, n_slab, has_aux):
    if has_aux:
        w2_ref, o_ref, o2_ref, hn_ref = rest
    else:
        o_ref, hn_ref = rest

    @pl.when(pl.program_id(1) == 0)
    def _():
        x = x_ref[...]
        ms = jnp.mean(x * x, axis=-1, keepdims=True)
        hn = ((x * lax.rsqrt(ms + RMS_EPS)) * g_ref[...]).astype(BF16)
        hn_ref[...] = hn
        if has_aux:
            o2_ref[...] = jnp.dot(hn, w2_ref[...], preferred_element_type=F32)

    acc = jnp.dot(hn_ref[...], w_ref[...], preferred_element_type=F32) * cs_ref[...]
    for s in range(n_slab):
        o_ref[s] = acc[:, s * SLAB:(s + 1) * SLAB].astype(BF16)


def _norm_matmul(x, g, w, colscale, w_aux=None, *, tm=1024, tn=1024):
    t, d = x.shape
    n = w.shape[1]
    tm = min(tm, t)
    assert t % tm == 0 and n % tn == 0 and tn % SLAB == 0
    n_slab = tn // SLAB
    has_aux = w_aux is not None
    in_specs = [
        pl.BlockSpec((tm, d), lambda i, j: (i, 0)),
        pl.BlockSpec((1, d), lambda i, j: (0, 0)),
        pl.BlockSpec((d, tn), lambda i, j: (0, j)),
        pl.BlockSpec((1, tn), lambda i, j: (0, j)),
    ]
    out_shape = [jax.ShapeDtypeStruct((n // SLAB, t, SLAB), BF16)]
    out_specs = [pl.BlockSpec((n_slab, tm, SLAB), lambda i, j: (j, i, 0))]
    args = [x, g.reshape(1, d), w, colscale.reshape(1, n)]
    if has_aux:
        in_specs.append(pl.BlockSpec((d, LANES), lambda i, j: (0, 0)))
        out_shape.append(jax.ShapeDtypeStruct((t, LANES), F32))
        out_specs.append(pl.BlockSpec((tm, LANES), lambda i, j: (i, 0)))
        args.append(w_aux)
    res = pl.pallas_call(
        functools.partial(_norm_matmul_kernel, n_slab=n_slab, has_aux=has_aux),
        grid=(t // tm, n // tn),
        in_specs=in_specs,
        out_specs=out_specs,
        out_shape=out_shape,
        scratch_shapes=[pltpu.VMEM((tm, d), BF16)],
        compiler_params=_cparams(("parallel", "arbitrary")),
    )(*args)
    return res if has_aux else res[0]


def _gla_kernel(q_ref, k_ref, v_ref, r_ref, gz_ref, wg_ref, bg_ref, ng_ref, o_ref,
                state_ref, la_ref, ob_ref, *, n_blocks):
    @pl.when(pl.program_id(2) == 0)
    def _():
        state_ref[...] = jnp.zeros_like(state_ref)

    def split3(x):
        hi = x.astype(BF16)
        r1 = x - hi.astype(F32)
        mid = r1.astype(BF16)
        return hi, mid, (r1 - mid.astype(F32)).astype(BF16)

    def dot32(a, b):
        return jnp.dot(a, b, preferred_element_type=F32)

    gh, gm, _ = split3(gz_ref[...])
    wh, wm, _ = split3(wg_ref[...])
    z = dot32(gh, wh) + dot32(gm, wh) + dot32(gh, wm) + bg_ref[...]
    la_ref[...] = -(jnp.maximum(-z, 0.0) + jnp.log1p(jnp.exp(-jnp.abs(z)))) * (1.0 / GLA_GATE_TAU)

    nb = GLA_BLOCK
    nc = nb // CHUNK
    row = lax.broadcasted_iota(I32, (nb, nb), 0)
    col = lax.broadcasted_iota(I32, (nb, nb), 1)
    same_chunk = (row // CHUNK) == (col // CHUNK)
    later = (same_chunk & (col > row)).astype(BF16)
    visible = (row // CHUNK) >= (col // CHUNK)
    dk = state_ref.shape[0]
    dv = state_ref.shape[1]
    nt = (((1,), (1,)), ((), ()))

    def per_chunk_rows(vals):
        return jnp.concatenate([jnp.broadcast_to(x, (CHUNK, dk)) for x in vals], axis=0)

    def body(i, carry):
        sl = pl.ds(pl.multiple_of(i * nb, nb), nb)
        la = la_ref[sl, :]
        suf = sum(dot32(later, t) for t in split3(la))
        tot = [suf[c * CHUNK:c * CHUNK + 1] + la[c * CHUNK:c * CHUNK + 1] for c in range(nc)]
        a = [tot[0]]
        for c in range(1, nc):
            a.append(a[-1] + tot[c])
        qf = q_ref[0, sl, :].astype(F32)
        k_dec = k_ref[0, sl, :].astype(F32) * jnp.exp(suf)
        v = jnp.concatenate([v_ref[0, sl, :], v_ref[1, sl, :]], axis=-1)
        s0 = state_ref[...]

        p = None
        zeros = jnp.zeros((CHUNK, dk), BF16)
        k_dec16 = k_dec.astype(BF16)
        for i_c in range(nc):
            q_i = (qf * per_chunk_rows([jnp.exp(jnp.minimum(a[c] - a[i_c], 0.0))
                                        for c in range(nc)])).astype(BF16)
            k_i = jnp.concatenate([k_dec16[c * CHUNK:(c + 1) * CHUNK] if c == i_c else zeros
                                   for c in range(nc)], axis=0)
            term = lax.dot_general(q_i, k_i, nt, preferred_element_type=F32)
            p = term if p is None else p + term
        p = jnp.where(visible, p, 0.0).astype(BF16)
        q_in = (qf * per_chunk_rows([jnp.exp(a[c]) for c in range(nc)])).astype(BF16)
        ob_ref[sl, :] = dot32(q_in, s0.astype(BF16)) + dot32(p, v)

        k_out = (k_dec * per_chunk_rows([jnp.exp(a[nc - 1] - a[c]) for c in range(nc)])).astype(BF16)
        upd = lax.dot_general(k_out, v, (((0,), (0,)), ((), ())), preferred_element_type=F32)
        decay = jnp.exp(jnp.broadcast_to(a[nc - 1], (LANES, dk)).T)
        state_ref[...] = jnp.concatenate([decay] * (dv // LANES), axis=-1) * s0 + upd
        return carry

    lax.fori_loop(0, n_blocks, body, 0)

    o = ob_ref[...]
    ms = jnp.mean(o * o, axis=-1, keepdims=True)
    on = (o * lax.rsqrt(ms + RMS_EPS)) * ng_ref[...]
    r = jnp.concatenate([r_ref[0], r_ref[1]], axis=-1).astype(F32)
    o_ref[...] = (on * (r * jax.nn.sigmoid(r))).astype(BF16)


def _gla_scan(proj, gz, w_gate_up, b_gate, norm_g, *, batch, seq, lc=1024):
    t = batch * seq
    lc = min(lc, seq)
    assert seq % lc == 0 and lc % GLA_BLOCK == 0 and GLA_BLOCK % CHUNK == 0
    ncb = seq // lc
    dk = SLAB
    dv = 2 * SLAB
    d_out = GLA_HEADS * dv
    rank = w_gate_up.shape[0]
    wg = jnp.zeros((LANES, GLA_HEADS * dk), F32).at[:rank].set(w_gate_up)

    def tok(b, h, c):
        return b * ncb + c

    return pl.pallas_call(
        functools.partial(_gla_kernel, n_blocks=lc // GLA_BLOCK),
        grid=(batch, GLA_HEADS, ncb),
        in_specs=[
            pl.BlockSpec((1, lc, SLAB), lambda b, h, c: (h, tok(b, h, c), 0)),
            pl.BlockSpec((1, lc, SLAB), lambda b, h, c: (GLA_HEADS + h, tok(b, h, c), 0)),
            pl.BlockSpec((2, lc, SLAB), lambda b, h, c: (GLA_HEADS + h, tok(b, h, c), 0)),
            pl.BlockSpec((2, lc, SLAB), lambda b, h, c: (2 * GLA_HEADS + h, tok(b, h, c), 0)),
            pl.BlockSpec((lc, LANES), lambda b, h, c: (tok(b, h, c), 0)),
            pl.BlockSpec((LANES, dk), lambda b, h, c: (0, h)),
            pl.BlockSpec((1, dk), lambda b, h, c: (0, h)),
            pl.BlockSpec((1, dv), lambda b, h, c: (0, 0)),
        ],
        out_specs=pl.BlockSpec((lc, dv), lambda b, h, c: (tok(b, h, c), h)),
        out_shape=jax.ShapeDtypeStruct((t, d_out), BF16),
        scratch_shapes=[pltpu.VMEM((dk, dv), F32), pltpu.VMEM((lc, dk), F32),
                        pltpu.VMEM((lc, dv), F32)],
        compiler_params=_cparams(("parallel", "parallel", "arbitrary")),
    )(proj, proj, proj, proj, gz, wg, b_gate.reshape(1, -1), norm_g.reshape(1, -1))


def _diff_attn_kernel(slope_ref, invct_ref, q_ref, k_ref, v_ref, qfeat_ref, kfeat_ref, lq1_ref, lk1_ref,
                      lq2_ref, lk2_ref, sg_ref, o_ref, vt_ref, dbias_ref, r0_ref, kmax_ref, m_ref, l_ref, acc_ref,
                      *, tq, lambda_init):
    h = pl.program_id(1)
    qi = pl.program_id(2)
    hd = SLAB // 2
    c = slope_ref[h] * LOG2E
    nt = (((1,), (1,)), ((), ()))

    ones16 = jnp.ones((16, hd), BF16)

    def sq_norms(x):
        xf = x.astype(F32)
        sq = (xf * xf * SQ_SLACK).astype(BF16)
        return lax.dot_general(ones16, sq, nt, preferred_element_type=F32)[0:1]

    @pl.when(qi == 0)
    def _():
        def per_block(i, carry):
            sl = pl.ds(pl.multiple_of(i * tq, tq), tq)
            vt_ref[:, sl] = v_ref[0, sl, :].T
            kb = k_ref[0, sl, :]
            return tuple(jnp.maximum(carry[u], sq_norms(kb[:, u * hd:(u + 1) * hd]))
                         for u in range(2))
        zero = jnp.zeros((1, tq), F32)
        kn2 = lax.fori_loop(0, v_ref.shape[1] // tq, per_block, (zero, zero))
        for u in range(2):
            kmax_ref[u] = jnp.broadcast_to(jnp.sqrt(jnp.max(kn2[u], axis=1, keepdims=True)),
                                           (8, LANES))
        ki = lax.broadcasted_iota(I32, (tq, tq), 0)
        qj = lax.broadcasted_iota(I32, (tq, tq), 1)
        bias = c * (qj - jnp.abs(qj - ki)).astype(F32)
        dbias_ref[...] = jnp.where((qj // CHUNK) >= (ki // CHUNK), bias, -jnp.inf)

    q = q_ref[0]
    start = pl.multiple_of(qi * tq, tq)

    k = k_ref[0, pl.ds(start, tq), :]
    vt = vt_ref[:, pl.ds(start, tq)]
    for u in range(2):
        r = lax.dot_general(k[:, u * hd:(u + 1) * hd], q[:, u * hd:(u + 1) * hd], nt,
                            preferred_element_type=F32) + dbias_ref[...]
        m_new = jnp.max(r, axis=0, keepdims=True)
        p = jnp.exp2(r - m_new)
        l_ref[u] = jnp.sum(p, axis=0, keepdims=True)
        acc_ref[u] = jnp.dot(vt, p.astype(BF16), preferred_element_type=F32)
        m_ref[u] = m_new

    qfeat = jnp.broadcast_to(qfeat_ref[0], (tq, LANES))
    kfeat = kfeat_ref[...]
    qa = [jnp.concatenate([q[:, u * hd:(u + 1) * hd], qfeat], axis=-1) for u in range(2)]

    def scores(u, blk):
        st = pl.multiple_of(blk * tq, tq)
        ka = jnp.concatenate([k_ref[0, pl.ds(st, tq), u * hd:(u + 1) * hd], kfeat], axis=-1)
        return lax.dot_general(ka, qa[u], nt, preferred_element_type=F32)

    def accumulate(u, r, vt, off):
        m_prev = m_ref[u]
        m_new = jnp.maximum(m_prev, jnp.max(r, axis=0, keepdims=True) - off)
        alpha = jnp.exp2(m_prev - m_new)
        p = jnp.exp2(r - (m_new + off))
        l_ref[u] = alpha * l_ref[u] + jnp.sum(p, axis=0, keepdims=True)
        acc_ref[u] = alpha * acc_ref[u] + jnp.dot(vt, p.astype(BF16),
                                                  preferred_element_type=F32)
        m_ref[u] = m_new

    r0_ref[...] = scores(0, jnp.maximum(qi - 1, 0))

    def past_block(dd, carry):
        blk = qi - dd
        vt = vt_ref[:, pl.ds(pl.multiple_of(blk * tq, tq), tq)]
        off = c * (dd * tq).astype(F32)
        r1 = scores(1, blk)
        accumulate(0, r0_ref[...], vt, off)
        r0_ref[...] = scores(0, jnp.maximum(blk - 1, 0))
        accumulate(1, r1, vt, off)
        return carry

    gap = None
    for u in range(2):
        bound = (jnp.sqrt(sq_norms(q[:, u * hd:(u + 1) * hd])) * kmax_ref[u][0:1, 0:1]
                 * NORM_SLACK - m_ref[u])
        g = jnp.max(bound, axis=1, keepdims=True)
        gap = g if gap is None else jnp.maximum(gap, g)
    n_need = jnp.ceil((gap + EXP_ZERO) * invct_ref[h])
    n_need = jnp.clip(n_need, 0.0, float(v_ref.shape[1] // tq)).astype(I32)[0, 0]
    lax.fori_loop(1, jnp.minimum(qi, n_need) + 1, past_block, 0)

    lam = (jnp.exp(jnp.sum(lq1_ref[...] * lk1_ref[...], axis=-1, keepdims=True))
           - jnp.exp(jnp.sum(lq2_ref[...] * lk2_ref[...], axis=-1, keepdims=True))
           + lambda_init)
    o = acc_ref[0] * (1.0 / l_ref[0]) - lam * (acc_ref[1] * (1.0 / l_ref[1]))
    o = o.T
    ms = jnp.mean(o * o, axis=-1, keepdims=True)
    o = (o * lax.rsqrt(ms + RMS_EPS)) * sg_ref[...]
    o_ref[...] = (o * (1.0 - lambda_init)).astype(BF16)


def _alibi_features(slopes, tq):
    c = slopes * LOG2E
    c_hi = c.astype(BF16).astype(F32)
    c_lo = c - c_hi
    qf = jnp.zeros((slopes.shape[0], 1, LANES), F32)
    qf = qf.at[:, 0, 0].set(LANES * c_hi).at[:, 0, 1].set(c_hi)
    qf = qf.at[:, 0, 2].set(LANES * c_lo).at[:, 0, 3].set(c_lo)
    idx = jnp.arange(tq, dtype=I32)
    hi = (idx // LANES).astype(F32)
    lo = (idx % LANES).astype(F32)
    kf = jnp.zeros((tq, LANES), F32).at[:, 0].set(hi).at[:, 1].set(lo).at[:, 2].set(hi).at[:, 3].set(lo)
    return qf.astype(BF16), kf.astype(BF16)


def _diff_attention(qkv, lq1, lk1, lq2, lk2, subln, *, batch, seq, layer_idx, tq=512):
    t = batch * seq
    tq = min(tq, seq)
    assert seq % tq == 0 and tq % CHUNK == 0
    nq = seq // tq
    lambda_init = 0.8 - 0.6 * math.exp(-0.3 * layer_idx)
    slopes = 2.0 ** (-8.0 * jnp.arange(1, DIFF_HEADS + 1, dtype=F32) / DIFF_HEADS)
    vec = lambda a: a.reshape(1, -1).astype(F32)
    qfeat, kfeat = _alibi_features(slopes, tq)
    grid_spec = pltpu.PrefetchScalarGridSpec(
        num_scalar_prefetch=2,
        grid=(batch, DIFF_HEADS, nq),
        in_specs=[
            pl.BlockSpec((1, tq, SLAB), lambda b, h, i, s, n: (h, b * nq + i, 0)),
            pl.BlockSpec((1, seq, SLAB), lambda b, h, i, s, n: (DIFF_HEADS + h, b, 0)),
            pl.BlockSpec((1, seq, SLAB), lambda b, h, i, s, n: (2 * DIFF_HEADS + h, b, 0)),
            pl.BlockSpec((1, 1, LANES), lambda b, h, i, s, n: (h, 0, 0)),
            pl.BlockSpec((tq, LANES), lambda b, h, i, s, n: (0, 0)),
            pl.BlockSpec((1, SLAB // 2), lambda b, h, i, s, n: (0, 0)),
            pl.BlockSpec((1, SLAB // 2), lambda b, h, i, s, n: (0, 0)),
            pl.BlockSpec((1, SLAB // 2), lambda b, h, i, s, n: (0, 0)),
            pl.BlockSpec((1, SLAB // 2), lambda b, h, i, s, n: (0, 0)),
            pl.BlockSpec((1, SLAB), lambda b, h, i, s, n: (0, 0)),
        ],
        out_specs=pl.BlockSpec((tq, SLAB), lambda b, h, i, s, n: (b * nq + i, h)),
        scratch_shapes=[pltpu.VMEM((SLAB, seq), BF16), pltpu.VMEM((tq, tq), F32),
                        pltpu.VMEM((tq, tq), F32), pltpu.VMEM((2, 8, LANES), F32),
                        pltpu.VMEM((2, 1, tq), F32), pltpu.VMEM((2, 1, tq), F32),
                        pltpu.VMEM((2, SLAB, tq), F32)],
    )
    return pl.pallas_call(
        functools.partial(_diff_attn_kernel, tq=tq, lambda_init=lambda_init),
        grid_spec=grid_spec,
        out_shape=jax.ShapeDtypeStruct((t, DIFF_HEADS * SLAB), BF16),
        compiler_params=_cparams(("parallel", "parallel", "arbitrary")),
    )(slopes, 1.0 / (slopes * (LOG2E * tq)), qkv, qkv, qkv, qfeat, kfeat, vec(lq1), vec(lk1), vec(lq2), vec(lk2), vec(subln))


def _proj_router_kernel(o_ref, w_ref, x_ref, g_ref, wr_ref, x1_ref, ri_ref, rw_ref):
    x1 = x_ref[...] + jnp.dot(o_ref[...], w_ref[...], preferred_element_type=F32)
    x1_ref[...] = x1
    ms = jnp.mean(x1 * x1, axis=-1, keepdims=True)
    hn = ((x1 * lax.rsqrt(ms + RMS_EPS)) * g_ref[...]).astype(BF16)
    logits = jnp.dot(hn, wr_ref[...], preferred_element_type=F32)
    lane = lax.broadcasted_iota(I32, logits.shape, 1)
    neg = -jnp.inf

    def first_max(vals):
        m = jnp.max(vals, axis=-1, keepdims=True)
        idx = jnp.min(jnp.where(vals == m, lane, LANES), axis=-1, keepdims=True)
        return m, idx

    gmask = lane < MOE_GROUPS
    gmax, gidx = first_max(jnp.where(gmask, logits, neg))
    gden = jnp.sum(jnp.where(gmask, jnp.exp(logits - gmax), 0.0), axis=-1, keepdims=True)
    g_w = 1.0 / gden
    lo = MOE_GROUPS + MOE_EXPERTS_PER_GROUP * gidx
    emask = (lane >= lo) & (lane < lo + MOE_EXPERTS_PER_GROUP)
    elog = jnp.where(emask, logits, neg)
    m1, i1 = first_max(elog)
    m2, i2 = first_max(jnp.where(lane == i1, neg, elog))
    tt = jnp.exp(m2 - m1)
    w1 = g_w / (1.0 + tt)
    w2 = w1 * tt
    ri_ref[...] = jnp.where(lane == 0, i1 - MOE_GROUPS, jnp.where(lane == 1, i2 - MOE_GROUPS, 0))
    rw_ref[...] = jnp.where(lane == 0, w1, jnp.where(lane == 1, w2, 0.0))


def _proj_router(o, w_out, x, g_ffn, w_router, *, tm=512):
    t, d = x.shape
    tm = min(tm, t)
    assert t % tm == 0
    return pl.pallas_call(
        _proj_router_kernel,
        grid=(t // tm,),
        in_specs=[
            pl.BlockSpec((tm, d), lambda i: (i, 0)),
            pl.BlockSpec((d, d), lambda i: (0, 0)),
            pl.BlockSpec((tm, d), lambda i: (i, 0)),
            pl.BlockSpec((1, d), lambda i: (0, 0)),
            pl.BlockSpec((d, LANES), lambda i: (0, 0)),
        ],
        out_specs=[
            pl.BlockSpec((tm, d), lambda i: (i, 0)),
            pl.BlockSpec((tm, LANES), lambda i: (i, 0)),
            pl.BlockSpec((tm, LANES), lambda i: (i, 0)),
        ],
        out_shape=[jax.ShapeDtypeStruct((t, d), F32), jax.ShapeDtypeStruct((t, LANES), I32),
                   jax.ShapeDtypeStruct((t, LANES), F32)],
        compiler_params=_cparams(("parallel",)),
    )(o, w_out, x, g_ffn.reshape(1, d), w_router)


def _gather_rows(idx_ref, n, src_hbm, dst_ref, sem):
    def body(r, carry):
        t = idx_ref[0, 0, r]
        pltpu.make_async_copy(src_hbm.at[pl.ds(t, 1), :], dst_ref.at[pl.ds(r, 1), :], sem).start()
        return carry
    lax.fori_loop(0, n, body, 0, unroll=8)


def _wait_rows(n, src_hbm, dst_ref, sem):
    pltpu.make_async_copy(src_hbm.at[pl.ds(0, n), :], dst_ref, sem).wait()


def _moe_kernel(eid_ref, tok_ref, tok_next_ref, x_hbm, g_ref, wg32_ref, wu32_ref,
                wd32_ref, o_ref, xbuf, sem, wg_ref, wu_ref, wd_ref, *, blk):
    b = pl.program_id(0)
    nb = pl.num_programs(0)
    slot = b % 2

    @pl.when((b == 0) | (eid_ref[b] != eid_ref[jnp.maximum(b - 1, 0)]))
    def _():
        wg_ref[0] = wg32_ref[0].astype(BF16)
        wu_ref[0] = wu32_ref[0].astype(BF16)
        wd_ref[0] = wd32_ref[0].astype(BF16)

    @pl.when(b == 0)
    def _():
        _gather_rows(tok_ref, blk, x_hbm, xbuf.at[0], sem.at[0])

    _wait_rows(blk, x_hbm, xbuf.at[slot], sem.at[slot])

    x = xbuf[slot]
    ms = jnp.mean(x * x, axis=-1, keepdims=True)
    hn = ((x * lax.rsqrt(ms + RMS_EPS)) * g_ref[...]).astype(BF16)

    for r in range(blk):
        pltpu.make_async_copy(x_hbm.at[pl.ds(tok_next_ref[0, 0, r], 1), :],
                              xbuf.at[1 - slot, pl.ds(r, 1), :], sem.at[1 - slot]).start()

    hg = jnp.dot(hn, wg_ref[0], preferred_element_type=F32)
    hu = jnp.dot(hn, wu_ref[0], preferred_element_type=F32)
    act = ((hg * jax.nn.sigmoid(hg)) * hu).astype(BF16)
    o_ref[...] = jnp.dot(act, wd_ref[0], preferred_element_type=F32)

    @pl.when(b == nb - 1)
    def _():
        _wait_rows(blk, x_hbm, xbuf.at[1 - slot], sem.at[1 - slot])


def _moe_ffn(x, g_ffn, blk_eid, row_tok, w_gate, w_up, w_down, *, blk, layer):
    t, d = x.shape
    n_rows = row_tok.shape[0]
    n_blocks = n_rows // blk
    dff = w_gate.shape[-1]
    first = layer * MOE_N_EXPERTS
    tok3 = row_tok.reshape(n_blocks, 1, blk)
    grid_spec = pltpu.PrefetchScalarGridSpec(
        num_scalar_prefetch=1,
        grid=(n_blocks,),
        in_specs=[
            pl.BlockSpec((1, 1, blk), lambda b, e: (b, 0, 0), memory_space=pltpu.SMEM),
            pl.BlockSpec((1, 1, blk), lambda b, e: (jnp.minimum(b + 1, n_blocks - 1), 0, 0),
                         memory_space=pltpu.SMEM),
            pl.BlockSpec(memory_space=pl.ANY),
            pl.BlockSpec((1, d), lambda b, e: (0, 0)),
            pl.BlockSpec((1, d, dff), lambda b, e: (first + e[b], 0, 0)),
            pl.BlockSpec((1, d, dff), lambda b, e: (first + e[b], 0, 0)),
            pl.BlockSpec((1, dff, d), lambda b, e: (first + e[b], 0, 0)),
        ],
        out_specs=pl.BlockSpec((blk, d), lambda b, e: (b, 0)),
        scratch_shapes=[pltpu.VMEM((2, blk, d), F32), pltpu.SemaphoreType.DMA((2,)),
                        pltpu.VMEM((1, d, dff), BF16), pltpu.VMEM((1, d, dff), BF16),
                        pltpu.VMEM((1, dff, d), BF16)],
    )
    return pl.pallas_call(
        functools.partial(_moe_kernel, blk=blk),
        grid_spec=grid_spec,
        out_shape=jax.ShapeDtypeStruct((n_rows, d), F32),
        compiler_params=_cparams(("arbitrary",)),
    )(blk_eid, tok3, tok3, x, g_ffn.reshape(1, d), w_gate, w_up, w_down)


def _combine_kernel(pos_ref, pos_next_ref, x_ref, rw_ref, y_hbm, g_ref, o_ref, ybuf, sem,
                    *, tc, final_norm):
    i = pl.program_id(0)
    n = pl.num_programs(0)
    slot = i % 2

    @pl.when(i == 0)
    def _():
        _gather_rows(pos_ref, 2 * tc, y_hbm, ybuf.at[0], sem.at[0])

    @pl.when(i + 1 < n)
    def _():
        _gather_rows(pos_next_ref, 2 * tc, y_hbm, ybuf.at[1 - slot], sem.at[1 - slot])

    _wait_rows(2 * tc, y_hbm, ybuf.at[slot], sem.at[slot])

    rw = rw_ref[...]
    y = x_ref[...] + rw[:, 0:1] * ybuf[slot, pl.ds(0, tc), :] + rw[:, 1:2] * ybuf[slot, pl.ds(tc, tc), :]
    if final_norm:
        ms = jnp.mean(y * y, axis=-1, keepdims=True)
        y = (y * lax.rsqrt(ms + RMS_EPS)) * g_ref[...]
    o_ref[...] = y


def _combine(x, route_w, pos, y_sorted, g_final, *, final_norm, tc=256):
    t, d = x.shape
    tc = min(tc, t)
    assert t % tc == 0
    nt = t // tc
    pos3 = pos.reshape(nt, tc, 2).transpose(0, 2, 1).reshape(nt, 1, 2 * tc)
    return pl.pallas_call(
        functools.partial(_combine_kernel, tc=tc, final_norm=final_norm),
        grid=(nt,),
        in_specs=[
            pl.BlockSpec((1, 1, 2 * tc), lambda i: (i, 0, 0), memory_space=pltpu.SMEM),
            pl.BlockSpec((1, 1, 2 * tc), lambda i: (jnp.minimum(i + 1, nt - 1), 0, 0),
                         memory_space=pltpu.SMEM),
            pl.BlockSpec((tc, d), lambda i: (i, 0)),
            pl.BlockSpec((tc, LANES), lambda i: (i, 0)),
            pl.BlockSpec(memory_space=pl.ANY),
            pl.BlockSpec((1, d), lambda i: (0, 0)),
        ],
        out_specs=pl.BlockSpec((tc, d), lambda i: (i, 0)),
        out_shape=jax.ShapeDtypeStruct((t, d), F32),
        scratch_shapes=[pltpu.VMEM((2, 2 * tc, d), F32), pltpu.SemaphoreType.DMA((2,))],
        compiler_params=_cparams(("arbitrary",)),
    )(pos3, pos3, x, route_w, y_sorted, g_final.reshape(1, d))


def _dispatch(route_i, blk):
    t = route_i.shape[0]
    eid = route_i[:, :2].reshape(-1)
    n_assign = eid.shape[0]
    onehot = (eid[:, None] == jnp.arange(MOE_N_EXPERTS, dtype=I32)[None, :]).astype(F32)
    tile = 256
    oh = onehot.reshape(n_assign // tile, tile, MOE_N_EXPERTS)
    within = jnp.einsum('ij,tje->tie', jnp.tril(jnp.ones((tile, tile), F32)), oh)
    tile_tot = within[:, -1, :]
    tile_off = jnp.cumsum(tile_tot, axis=0) - tile_tot
    csum = within + tile_off[:, None, :]
    rank = (jnp.sum(csum * oh, axis=2) - 1.0).reshape(n_assign).astype(I32)
    counts = (tile_off[-1] + tile_tot[-1]).astype(I32)
    padded = ((counts + blk - 1) // blk) * blk
    pend = jnp.cumsum(padded)
    pstart = pend - padded
    dest = (pstart[eid] + rank).astype(I32)
    n_rows = n_assign + MOE_N_EXPERTS * blk
    n_blocks = n_rows // blk
    row_tok = jnp.zeros((n_rows,), I32).at[dest].set(jnp.arange(n_assign, dtype=I32) // 2)
    blk_start = jnp.arange(n_blocks, dtype=I32) * blk
    blk_eid = jnp.minimum(jnp.sum((pend[None, :] <= blk_start[:, None]).astype(I32), axis=1),
                          MOE_N_EXPERTS - 1)
    return row_tok, blk_eid, dest.reshape(t, 2)


def _router_weights(w_group, w_expert):
    d = w_group.shape[0]
    w = jnp.zeros((d, LANES), F32)
    w = w.at[:, :MOE_GROUPS].set(w_group).at[:, MOE_GROUPS:MOE_GROUPS + MOE_N_EXPERTS].set(w_expert)
    return w.astype(BF16)


def _moe_layer(x1, route_i, route_w, g_ffn, w_gate, w_up, w_down, g_final, *, layer, final_norm,
               blk=256):
    row_tok, blk_eid, pos = _dispatch(route_i, blk)
    stack = lambda w: w.reshape((-1,) + w.shape[2:])
    y_sorted = _moe_ffn(x1, g_ffn, blk_eid, row_tok, stack(w_gate), stack(w_up),
                        stack(w_down), blk=blk, layer=layer)
    return _combine(x1, route_w, pos, y_sorted, g_final, final_norm=final_norm)


def kernel(x, norm_mix, norm_ffn, gla_w_in, gla_w_gate_up, gla_b_gate, gla_norm, gla_w_out,
           diff_w_in, diff_lambda_q1, diff_lambda_k1, diff_lambda_q2, diff_lambda_k2,
           diff_subln, diff_w_out, moe_w_group, moe_w_expert, moe_w_gate, moe_w_up,
           moe_w_down, final_norm):
    batch, seq, d = x.shape
    t = batch * seq
    xf = x.reshape(t, d)

    dk_total = d // 2
    n_main = 2 * dk_total + 2 * d
    w_in = gla_w_in[0]
    gla_dk = dk_total // GLA_HEADS
    colscale = jnp.ones((n_main,), F32).at[:dk_total].set(gla_dk ** -0.5)
    rank = w_in.shape[1] - n_main
    w_gz = jnp.zeros((d, LANES), F32).at[:, :rank].set(w_in[:, n_main:]).astype(BF16)
    proj, gz = _norm_matmul(xf, norm_mix[0], w_in[:, :n_main].astype(BF16), colscale, w_gz)
    o = _gla_scan(proj, gz, gla_w_gate_up[0], gla_b_gate[0], gla_norm[0], batch=batch, seq=seq)
    x1, ri, rw = _proj_router(o, gla_w_out[0].astype(BF16), xf, norm_ffn[0],
                              _router_weights(moe_w_group[0], moe_w_expert[0]))
    x2 = _moe_layer(x1, ri, rw, norm_ffn[0], moe_w_gate, moe_w_up, moe_w_down,
                    final_norm, layer=0, final_norm=False)

    hd = d // DIFF_HEADS // 2
    colscale = jnp.ones((3 * d,), F32).at[:d].set(hd ** -0.5 * LOG2E)
    qkv = _norm_matmul(x2, norm_mix[1], diff_w_in[0].astype(BF16), colscale)
    o = _diff_attention(qkv, diff_lambda_q1[0], diff_lambda_k1[0], diff_lambda_q2[0],
                        diff_lambda_k2[0], diff_subln[0], batch=batch, seq=seq, layer_idx=1)
    x3, ri, rw = _proj_router(o, diff_w_out[0].astype(BF16), x2, norm_ffn[1],
                              _router_weights(moe_w_group[1], moe_w_expert[1]))
    out = _moe_layer(x3, ri, rw, norm_ffn[1], moe_w_gate, moe_w_up, moe_w_down,
                     final_norm, layer=1, final_norm=True)
    return out.reshape(batch, seq, d)
```

```python
import functools
import math

import jax
import jax.numpy as jnp
from jax import lax
from jax.experimental import pallas as pl
from jax.experimental.pallas import tpu as pltpu

F32 = jnp.float32
BF16 = jnp.bfloat16
I32 = jnp.int32

RMS_EPS = 1e-6
CHUNK = 64
GLA_BLOCK = 256
GLA_HEADS = 4
GLA_GATE_TAU = 16.0
DIFF_HEADS = 8
MOE_GROUPS = 4
MOE_EXPERTS_PER_GROUP = 8
MOE_N_EXPERTS = MOE_GROUPS * MOE_EXPERTS_PER_GROUP

LANES = 128
SLAB = 256
VMEM_LIMIT = 56 * 1024 * 1024
LOG2E = 1.4426950408889634
NEG_BIG = -1e30
EXP_ZERO = 160.0
NORM_SLACK = 1.0 + 2.0 ** -10
SQ_SLACK = 1.0 + 2.0 ** -7

HIGHEST = lax.Precision.HIGHEST


def _cparams(sem):
    return pltpu.CompilerParams(dimension_semantics=sem, vmem_limit_bytes=VMEM_LIMIT)


def _norm_matmul_kernel(x_ref, g_ref, w_ref, cs_ref, *rest, n_slab, has_aux):
    if has_aux:
        w2_ref, o_ref, o2_ref, hn_ref = rest
    else:
        o_ref, hn_ref = rest

    @pl.when(pl.program_id(1) == 0)
    def _():
        x = x_ref[...]
        ms = jnp.mean(x * x, axis=-1, keepdims=True)
        hn = ((x * lax.rsqrt(ms + RMS_EPS)) * g_ref[...]).astype(BF16)
        hn_ref[...] = hn
        if has_aux:
            o2_ref[...] = jnp.dot(hn, w2_ref[...], preferred_element_type=F32)

    acc = jnp.dot(hn_ref[...], w_ref[...], preferred_element_type=F32) * cs_ref[...]
    for s in range(n_slab):
        o_ref[s] = acc[:, s * SLAB:(s + 1) * SLAB].astype(BF16)


def _norm_matmul(x, g, w, colscale, w_aux=None, *, tm=1024, tn=1024):
    t, d = x.shape
    n = w.shape[1]
    tm = min(tm, t)
    assert t % tm == 0 and n % tn == 0 and tn % SLAB == 0
    n_slab = tn // SLAB
    has_aux = w_aux is not None
    in_specs = [
        pl.BlockSpec((tm, d), lambda i, j: (i, 0)),
        pl.BlockSpec((1, d), lambda i, j: (0, 0)),
        pl.BlockSpec((d, tn), lambda i, j: (0, j)),
        pl.BlockSpec((1, tn), lambda i, j: (0, j)),
    ]
    out_shape = [jax.ShapeDtypeStruct((n // SLAB, t, SLAB), BF16)]
    out_specs = [pl.BlockSpec((n_slab, tm, SLAB), lambda i, j: (j, i, 0))]
    args = [x, g.reshape(1, d), w, colscale.reshape(1, n)]
    if has_aux:
        in_specs.append(pl.BlockSpec((d, LANES), lambda i, j: (0, 0)))
        out_shape.append(jax.ShapeDtypeStruct((t, LANES), F32))
        out_specs.append(pl.BlockSpec((tm, LANES), lambda i, j: (i, 0)))
        args.append(w_aux)
    res = pl.pallas_call(
        functools.partial(_norm_matmul_kernel, n_slab=n_slab, has_aux=has_aux),
        grid=(t // tm, n // tn),
        in_specs=in_specs,
        out_specs=out_specs,
        out_shape=out_shape,
        scratch_shapes=[pltpu.VMEM((tm, d), BF16)],
        compiler_params=_cparams(("parallel", "arbitrary")),
    )(*args)
    return res if has_aux else res[0]


def _gla_kernel(q_ref, k_ref, v_ref, r_ref, gz_ref, wg_ref, bg_ref, ng_ref, o_ref,
                state_ref, la_ref, ob_ref, *, n_blocks):
    @pl.when(pl.program_id(2) == 0)
    def _():
        state_ref[...] = jnp.zeros_like(state_ref)

    def split3(x):
        hi = x.astype(BF16)
        r1 = x - hi.astype(F32)
        mid = r1.astype(BF16)
        return hi, mid, (r1 - mid.astype(F32)).astype(BF16)

    def dot32(a, b):
        return jnp.dot(a, b, preferred_element_type=F32)

    gh, gm, _ = split3(gz_ref[...])
    wh, wm, _ = split3(wg_ref[...])
    z = dot32(gh, wh) + dot32(gm, wh) + dot32(gh, wm) + bg_ref[...]
    la_ref[...] = -(jnp.maximum(-z, 0.0) + jnp.log1p(jnp.exp(-jnp.abs(z)))) * (1.0 / GLA_GATE_TAU)

    nb = GLA_BLOCK
    nc = nb // CHUNK
    row = lax.broadcasted_iota(I32, (nb, nb), 0)
    col = lax.broadcasted_iota(I32, (nb, nb), 1)
    same_chunk = (row // CHUNK) == (col // CHUNK)
    later = (same_chunk & (col > row)).astype(BF16)
    visible = (row // CHUNK) >= (col // CHUNK)
    dk = state_ref.shape[0]
    dv = state_ref.shape[1]
    nt = (((1,), (1,)), ((), ()))

    def per_chunk_rows(vals):
        return jnp.concatenate([jnp.broadcast_to(x, (CHUNK, dk)) for x in vals], axis=0)

    def body(i, carry):
        sl = pl.ds(pl.multiple_of(i * nb, nb), nb)
        la = la_ref[sl, :]
        suf = sum(dot32(later, t) for t in split3(la))
        tot = [suf[c * CHUNK:c * CHUNK + 1] + la[c * CHUNK:c * CHUNK + 1] for c in range(nc)]
        a = [tot[0]]
        for c in range(1, nc):
            a.append(a[-1] + tot[c])
        qf = q_ref[0, sl, :].astype(F32)
        k_dec = k_ref[0, sl, :].astype(F32) * jnp.exp(suf)
        v = jnp.concatenate([v_ref[0, sl, :], v_ref[1, sl, :]], axis=-1)
        s0 = state_ref[...]

        p = None
        zeros = jnp.zeros((CHUNK, dk), BF16)
        k_dec16 = k_dec.astype(BF16)
        for i_c in range(nc):
            q_i = (qf * per_chunk_rows([jnp.exp(jnp.minimum(a[c] - a[i_c], 0.0))
                                        for c in range(nc)])).astype(BF16)
            k_i = jnp.concatenate([k_dec16[c * CHUNK:(c + 1) * CHUNK] if c == i_c else zeros
                                   for c in range(nc)], axis=0)
            term = lax.dot_general(q_i, k_i, nt, preferred_element_type=F32)
            p = term if p is None else p + term
        p = jnp.where(visible, p, 0.0).astype(BF16)
        q_in = (qf * per_chunk_rows([jnp.exp(a[c]) for c in range(nc)])).astype(BF16)
        ob_ref[sl, :] = dot32(q_in, s0.astype(BF16)) + dot32(p, v)

        k_out = (k_dec * per_chunk_rows([jnp.exp(a[nc - 1] - a[c]) for c in range(nc)])).astype(BF16)
        upd = lax.dot_general(k_out, v, (((0,), (0,)), ((), ())), preferred_element_type=F32)
        decay = jnp.exp(jnp.broadcast_to(a[nc - 1], (LANES, dk)).T)
        state_ref[...] = jnp.concatenate([decay] * (dv // LANES), axis=-1) * s0 + upd
        return carry

    lax.fori_loop(0, n_blocks, body, 0)

    o = ob_ref[...]
    ms = jnp.mean(o * o, axis=-1, keepdims=True)
    on = (o * lax.rsqrt(ms + RMS_EPS)) * ng_ref[...]
    r = jnp.concatenate([r_ref[0], r_ref[1]], axis=-1).astype(F32)
    o_ref[...] = (on * (r * jax.nn.sigmoid(r))).astype(BF16)


def _gla_scan(proj, gz, w_gate_up, b_gate, norm_g, *, batch, seq, lc=1024):
    t = batch * seq
    lc = min(lc, seq)
    assert seq % lc == 0 and lc % GLA_BLOCK == 0 and GLA_BLOCK % CHUNK == 0
    ncb = seq // lc
    dk = SLAB
    dv = 2 * SLAB
    d_out = GLA_HEADS * dv
    rank = w_gate_up.shape[0]
    wg = jnp.zeros((LANES, GLA_HEADS * dk), F32).at[:rank].set(w_gate_up)

    def tok(b, h, c):
        return b * ncb + c

    return pl.pallas_call(
        functools.partial(_gla_kernel, n_blocks=lc // GLA_BLOCK),
        grid=(batch, GLA_HEADS, ncb),
        in_specs=[
            pl.BlockSpec((1, lc, SLAB), lambda b, h, c: (h, tok(b, h, c), 0)),
            pl.BlockSpec((1, lc, SLAB), lambda b, h, c: (GLA_HEADS + h, tok(b, h, c), 0)),
            pl.BlockSpec((2, lc, SLAB), lambda b, h, c: (GLA_HEADS + h, tok(b, h, c), 0)),
            pl.BlockSpec((2, lc, SLAB), lambda b, h, c: (2 * GLA_HEADS + h, tok(b, h, c), 0)),
            pl.BlockSpec((lc, LANES), lambda b, h, c: (tok(b, h, c), 0)),
            pl.BlockSpec((LANES, dk), lambda b, h, c: (0, h)),
            pl.BlockSpec((1, dk), lambda b, h, c: (0, h)),
            pl.BlockSpec((1, dv), lambda b, h, c: (0, 0)),
        ],
        out_specs=pl.BlockSpec((lc, dv), lambda b, h, c: (tok(b, h, c), h)),
        out_shape=jax.ShapeDtypeStruct((t, d_out), BF16),
        scratch_shapes=[pltpu.VMEM((dk, dv), F32), pltpu.VMEM((lc, dk), F32),
                        pltpu.VMEM((lc, dv), F32)],
        compiler_params=_cparams(("parallel", "parallel", "arbitrary")),
    )(proj, proj, proj, proj, gz, wg, b_gate.reshape(1, -1), norm_g.reshape(1, -1))


def _diff_attn_kernel(slope_ref, invct_ref, q_ref, k_ref, v_ref, qfeat_ref, kfeat_ref, lq1_ref, lk1_ref,
                      lq2_ref, lk2_ref, sg_ref, o_ref, vt_ref, dbias_ref, r0_ref, kmax_ref, m_ref, l_ref, acc_ref,
                      *, tq, lambda_init):
    h = pl.program_id(1)
    qi = pl.program_id(2)
    hd = SLAB // 2
    c = slope_ref[h] * LOG2E
    nt = (((1,), (1,)), ((), ()))

    ones16 = jnp.ones((16, hd), BF16)

    def sq_norms(x):
        xf = x.astype(F32)
        sq = (xf * xf * SQ_SLACK).astype(BF16)
        return lax.dot_general(ones16, sq, nt, preferred_element_type=F32)[0:1]

    @pl.when(qi == 0)
    def _():
        def per_block(i, carry):
            sl = pl.ds(pl.multiple_of(i * tq, tq), tq)
            vt_ref[:, sl] = v_ref[0, sl, :].T
            kb = k_ref[0, sl, :]
            return tuple(jnp.maximum(carry[u], sq_norms(kb[:, u * hd:(u + 1) * hd]))
                         for u in range(2))
        zero = jnp.zeros((1, tq), F32)
        kn2 = lax.fori_loop(0, v_ref.shape[1] // tq, per_block, (zero, zero))
        for u in range(2):
            kmax_ref[u] = jnp.broadcast_to(jnp.sqrt(jnp.max(kn2[u], axis=1, keepdims=True)),
                                           (8, LANES))
        ki = lax.broadcasted_iota(I32, (tq, tq), 0)
        qj = lax.broadcasted_iota(I32, (tq, tq), 1)
        bias = c * (qj - jnp.abs(qj - ki)).astype(F32)
        dbias_ref[...] = jnp.where((qj // CHUNK) >= (ki // CHUNK), bias, -jnp.inf)

    q = q_ref[0]
    start = pl.multiple_of(qi * tq, tq)

    k = k_ref[0, pl.ds(start, tq), :]
    vt = vt_ref[:, pl.ds(start, tq)]
    for u in range(2):
        r = lax.dot_general(k[:, u * hd:(u + 1) * hd], q[:, u * hd:(u + 1) * hd], nt,
                            preferred_element_type=F32) + dbias_ref[...]
        m_new = jnp.max(r, axis=0, keepdims=True)
        p = jnp.exp2(r - m_new)
        l_ref[u] = jnp.sum(p, axis=0, keepdims=True)
        acc_ref[u] = jnp.dot(vt, p.astype(BF16), preferred_element_type=F32)
        m_ref[u] = m_new

    qfeat = jnp.broadcast_to(qfeat_ref[0], (tq, LANES))
    kfeat = kfeat_ref[...]
    qa = [jnp.concatenate([q[:, u * hd:(u + 1) * hd], qfeat], axis=-1) for u in range(2)]

    def scores(u, blk):
        st = pl.multiple_of(blk * tq, tq)
        ka = jnp.concatenate([k_ref[0, pl.ds(st, tq), u * hd:(u + 1) * hd], kfeat], axis=-1)
        return lax.dot_general(ka, qa[u], nt, preferred_element_type=F32)

    def accumulate(u, r, vt, off):
        m_prev = m_ref[u]
        m_new = jnp.maximum(m_prev, jnp.max(r, axis=0, keepdims=True) - off)
        alpha = jnp.exp2(m_prev - m_new)
        p = jnp.exp2(r - (m_new + off))
        l_ref[u] = alpha * l_ref[u] + jnp.sum(p, axis=0, keepdims=True)
        acc_ref[u] = alpha * acc_ref[u] + jnp.dot(vt, p.astype(BF16),
                                                  preferred_element_type=F32)
        m_ref[u] = m_new

    r0_ref[...] = scores(0, jnp.maximum(qi - 1, 0))

    def past_block(dd, carry):
        blk = qi - dd
        vt = vt_ref[:, pl.ds(pl.multiple_of(blk * tq, tq), tq)]
        off = c * (dd * tq).astype(F32)
        r1 = scores(1, blk)
        accumulate(0, r0_ref[...], vt, off)
        r0_ref[...] = scores(0, jnp.maximum(blk - 1, 0))
        accumulate(1, r1, vt, off)
        return carry

    gap = None
    for u in range(2):
        bound = (jnp.sqrt(sq_norms(q[:, u * hd:(u + 1) * hd])) * kmax_ref[u][0:1, 0:1]
                 * NORM_SLACK - m_ref[u])
        g = jnp.max(bound, axis=1, keepdims=True)
        gap = g if gap is None else jnp.maximum(gap, g)
    n_need = jnp.ceil((gap + EXP_ZERO) * invct_ref[h])
    n_need = jnp.clip(n_need, 0.0, float(v_ref.shape[1] // tq)).astype(I32)[0, 0]
    lax.fori_loop(1, jnp.minimum(qi, n_need) + 1, past_block, 0)

    lam = (jnp.exp(jnp.sum(lq1_ref[...] * lk1_ref[...], axis=-1, keepdims=True))
           - jnp.exp(jnp.sum(lq2_ref[...] * lk2_ref[...], axis=-1, keepdims=True))
           + lambda_init)
    o = acc_ref[0] * (1.0 / l_ref[0]) - lam * (acc_ref[1] * (1.0 / l_ref[1]))
    o = o.T
    ms = jnp.mean(o * o, axis=-1, keepdims=True)
    o = (o * lax.rsqrt(ms + RMS_EPS)) * sg_ref[...]
    o_ref[...] = (o * (1.0 - lambda_init)).astype(BF16)


def _alibi_features(slopes, tq):
    c = slopes * LOG2E
    c_hi = c.astype(BF16).astype(F32)
    c_lo = c - c_hi
    qf = jnp.zeros((slopes.shape[0], 1, LANES), F32)
    qf = qf.at[:, 0, 0].set(LANES * c_hi).at[:, 0, 1].set(c_hi)
    qf = qf.at[:, 0, 2].set(LANES * c_lo).at[:, 0, 3].set(c_lo)
    idx = jnp.arange(tq, dtype=I32)
    hi = (idx // LANES).astype(F32)
    lo = (idx % LANES).astype(F32)
    kf = jnp.zeros((tq, LANES), F32).at[:, 0].set(hi).at[:, 1].set(lo).at[:, 2].set(hi).at[:, 3].set(lo)
    return qf.astype(BF16), kf.astype(BF16)


def _diff_attention(qkv, lq1, lk1, lq2, lk2, subln, *, batch, seq, layer_idx, tq=512):
    t = batch * seq
    tq = min(tq, seq)
    assert seq % tq == 0 and tq % CHUNK == 0
    nq = seq // tq
    lambda_init = 0.8 - 0.6 * math.exp(-0.3 * layer_idx)
    slopes = 2.0 ** (-8.0 * jnp.arange(1, DIFF_HEADS + 1, dtype=F32) / DIFF_HEADS)
    vec = lambda a: a.reshape(1, -1).astype(F32)
    qfeat, kfeat = _alibi_features(slopes, tq)
    grid_spec = pltpu.PrefetchScalarGridSpec(
        num_scalar_prefetch=2,
        grid=(batch, DIFF_HEADS, nq),
        in_specs=[
            pl.BlockSpec((1, tq, SLAB), lambda b, h, i, s, n: (h, b * nq + i, 0)),
            pl.BlockSpec((1, seq, SLAB), lambda b, h, i, s, n: (DIFF_HEADS + h, b, 0)),
            pl.BlockSpec((1, seq, SLAB), lambda b, h, i, s, n: (2 * DIFF_HEADS + h, b, 0)),
            pl.BlockSpec((1, 1, LANES), lambda b, h, i, s, n: (h, 0, 0)),
            pl.BlockSpec((tq, LANES), lambda b, h, i, s, n: (0, 0)),
            pl.BlockSpec((1, SLAB // 2), lambda b, h, i, s, n: (0, 0)),
            pl.BlockSpec((1, SLAB // 2), lambda b, h, i, s, n: (0, 0)),
            pl.BlockSpec((1, SLAB // 2), lambda b, h, i, s, n: (0, 0)),
            pl.BlockSpec((1, SLAB // 2), lambda b, h, i, s, n: (0, 0)),
            pl.BlockSpec((1, SLAB), lambda b, h, i, s, n: (0, 0)),
        ],
        out_specs=pl.BlockSpec((tq, SLAB), lambda b, h, i, s, n: (b * nq + i, h)),
        scratch_shapes=[pltpu.VMEM((SLAB, seq), BF16), pltpu.VMEM((tq, tq), F32),
                        pltpu.VMEM((tq, tq), F32), pltpu.VMEM((2, 8, LANES), F32),
                        pltpu.VMEM((2, 1, tq), F32), pltpu.VMEM((2, 1, tq), F32),
                        pltpu.VMEM((2, SLAB, tq), F32)],
    )
    return pl.pallas_call(
        functools.partial(_diff_attn_kernel, tq=tq, lambda_init=lambda_init),
        grid_spec=grid_spec,
        out_shape=jax.ShapeDtypeStruct((t, DIFF_HEADS * SLAB), BF16),
        compiler_params=_cparams(("parallel", "parallel", "arbitrary")),
    )(slopes, 1.0 / (slopes * (LOG2E * tq)), qkv, qkv, qkv, qfeat, kfeat, vec(lq1), vec(lk1), vec(lq2), vec(lk2), vec(subln))


def _proj_router_kernel(o_ref, w_ref, x_ref, g_ref, wr_ref, x1_ref, ri_ref, rw_ref):
    x1 = x_ref[...] + jnp.dot(o_ref[...], w_ref[...], preferred_element_type=F32)
    x1_ref[...] = x1
    ms = jnp.mean(x1 * x1, axis=-1, keepdims=True)
    hn = ((x1 * lax.rsqrt(ms + RMS_EPS)) * g_ref[...]).astype(BF16)
    logits = jnp.dot(hn, wr_ref[...], preferred_element_type=F32)
    lane = lax.broadcasted_iota(I32, logits.shape, 1)
    neg = -jnp.inf

    def first_max(vals):
        m = jnp.max(vals, axis=-1, keepdims=True)
        idx = jnp.min(jnp.where(vals == m, lane, LANES), axis=-1, keepdims=True)
        return m, idx

    gmask = lane < MOE_GROUPS
    gmax, gidx = first_max(jnp.where(gmask, logits, neg))
    gden = jnp.sum(jnp.where(gmask, jnp.exp(logits - gmax), 0.0), axis=-1, keepdims=True)
    g_w = 1.0 / gden
    lo = MOE_GROUPS + MOE_EXPERTS_PER_GROUP * gidx
    emask = (lane >= lo) & (lane < lo + MOE_EXPERTS_PER_GROUP)
    elog = jnp.where(emask, logits, neg)
    m1, i1 = first_max(elog)
    m2, i2 = first_max(jnp.where(lane == i1, neg, elog))
    tt = jnp.exp(m2 - m1)
    w1 = g_w / (1.0 + tt)
    w2 = w1 * tt
    ri_ref[...] = jnp.where(lane == 0, i1 - MOE_GROUPS, jnp.where(lane == 1, i2 - MOE_GROUPS, 0))
    rw_ref[...] = jnp.where(lane == 0, w1, jnp.where(lane == 1, w2, 0.0))


def _proj_router(o, w_out, x, g_ffn, w_router, *, tm=512):
    t, d = x.shape
    tm = min(tm, t)
    assert t % tm == 0
    return pl.pallas_call(
        _proj_router_kernel,
        grid=(t // tm,),
        in_specs=[
            pl.BlockSpec((tm, d), lambda i: (i, 0)),
            pl.BlockSpec((d, d), lambda i: (0, 0)),
            pl.BlockSpec((tm, d), lambda i: (i, 0)),
            pl.BlockSpec((1, d), lambda i: (0, 0)),
            pl.BlockSpec((d, LANES), lambda i: (0, 0)),
        ],
        out_specs=[
            pl.BlockSpec((tm, d), lambda i: (i, 0)),
            pl.BlockSpec((tm, LANES), lambda i: (i, 0)),
            pl.BlockSpec((tm, LANES), lambda i: (i, 0)),
        ],
        out_shape=[jax.ShapeDtypeStruct((t, d), F32), jax.ShapeDtypeStruct((t, LANES), I32),
                   jax.ShapeDtypeStruct((t, LANES), F32)],
        compiler_params=_cparams(("parallel",)),
    )(o, w_out, x, g_ffn.reshape(1, d), w_router)


def _token_copy(src_hbm, dst_ref, sem, src_tok, dst_tok, rt):
    return pltpu.make_async_copy(src_hbm.at[pl.ds(pl.multiple_of(src_tok * rt, rt), rt), :],
                                 dst_ref.at[pl.ds(dst_tok * rt, rt), :], sem)


def _gather_rows(idx_ref, n, src_hbm, dst_ref, sem, rt=1):
    def body(r, carry):
        _token_copy(src_hbm, dst_ref, sem, idx_ref[0, 0, r], r, rt).start()
        return carry
    lax.fori_loop(0, n, body, 0, unroll=8)


def _wait_rows(n, src_hbm, dst_ref, sem, rt=1):
    pltpu.make_async_copy(src_hbm.at[pl.ds(0, n * rt), :], dst_ref, sem).wait()


def _moe_kernel(eid_ref, tok_ref, tok_next_ref, x_hbm, g_ref, wg32_ref, wu32_ref,
                wd32_ref, o_ref, xbuf, sem, wg_ref, wu_ref, wd_ref, *, blk, rt):
    b = pl.program_id(0)
    nb = pl.num_programs(0)
    slot = b % 2

    @pl.when((b == 0) | (eid_ref[b] != eid_ref[jnp.maximum(b - 1, 0)]))
    def _():
        wg_ref[0] = wg32_ref[0].astype(BF16)
        wu_ref[0] = wu32_ref[0].astype(BF16)
        wd_ref[0] = wd32_ref[0].astype(BF16)

    @pl.when(b == 0)
    def _():
        _gather_rows(tok_ref, blk, x_hbm, xbuf.at[0], sem.at[0], rt)

    _wait_rows(blk, x_hbm, xbuf.at[slot], sem.at[slot], rt)

    x = jnp.concatenate([xbuf[slot, pl.ds(c, blk, stride=rt), :] for c in range(rt)], axis=-1)
    ms = jnp.mean(x * x, axis=-1, keepdims=True)
    hn = ((x * lax.rsqrt(ms + RMS_EPS)) * g_ref[...]).astype(BF16)

    for r in range(blk):
        _token_copy(x_hbm, xbuf.at[1 - slot], sem.at[1 - slot], tok_next_ref[0, 0, r], r,
                    rt).start()

    hg = jnp.dot(hn, wg_ref[0], preferred_element_type=F32)
    hu = jnp.dot(hn, wu_ref[0], preferred_element_type=F32)
    act = ((hg * jax.nn.sigmoid(hg)) * hu).astype(BF16)
    o_ref[...] = jnp.dot(act, wd_ref[0], preferred_element_type=F32)

    @pl.when(b == nb - 1)
    def _():
        _wait_rows(blk, x_hbm, xbuf.at[1 - slot], sem.at[1 - slot], rt)


def _moe_ffn(x, g_ffn, blk_eid, row_tok, w_gate, w_up, w_down, *, blk, layer):
    t, d = x.shape
    n_rows = row_tok.shape[0]
    n_blocks = n_rows // blk
    dff = w_gate.shape[-1]
    first = layer * MOE_N_EXPERTS
    rt = d // LANES
    tok3 = row_tok.reshape(n_blocks, 1, blk)
    grid_spec = pltpu.PrefetchScalarGridSpec(
        num_scalar_prefetch=1,
        grid=(n_blocks,),
        in_specs=[
            pl.BlockSpec((1, 1, blk), lambda b, e: (b, 0, 0), memory_space=pltpu.SMEM),
            pl.BlockSpec((1, 1, blk), lambda b, e: (jnp.minimum(b + 1, n_blocks - 1), 0, 0),
                         memory_space=pltpu.SMEM),
            pl.BlockSpec(memory_space=pl.ANY),
            pl.BlockSpec((1, d), lambda b, e: (0, 0)),
            pl.BlockSpec((1, d, dff), lambda b, e: (first + e[b], 0, 0)),
            pl.BlockSpec((1, d, dff), lambda b, e: (first + e[b], 0, 0)),
            pl.BlockSpec((1, dff, d), lambda b, e: (first + e[b], 0, 0)),
        ],
        out_specs=pl.BlockSpec((blk, d), lambda b, e: (b, 0)),
        scratch_shapes=[pltpu.VMEM((2, blk * rt, LANES), F32), pltpu.SemaphoreType.DMA((2,)),
                        pltpu.VMEM((1, d, dff), BF16), pltpu.VMEM((1, d, dff), BF16),
                        pltpu.VMEM((1, dff, d), BF16)],
    )
    return pl.pallas_call(
        functools.partial(_moe_kernel, blk=blk, rt=rt),
        grid_spec=grid_spec,
        out_shape=jax.ShapeDtypeStruct((n_rows, d), F32),
        compiler_params=_cparams(("arbitrary",)),
    )(blk_eid, tok3, tok3, x.reshape(t * rt, LANES), g_ffn.reshape(1, d), w_gate, w_up, w_down)


def _combine_kernel(pos_ref, pos_next_ref, x_ref, rw_ref, y_hbm, g_ref, o_ref, ybuf, sem,
                    *, tc, final_norm):
    i = pl.program_id(0)
    n = pl.num_programs(0)
    slot = i % 2

    @pl.when(i == 0)
    def _():
        _gather_rows(pos_ref, 2 * tc, y_hbm, ybuf.at[0], sem.at[0])

    @pl.when(i + 1 < n)
    def _():
        _gather_rows(pos_next_ref, 2 * tc, y_hbm, ybuf.at[1 - slot], sem.at[1 - slot])

    _wait_rows(2 * tc, y_hbm, ybuf.at[slot], sem.at[slot])

    rw = rw_ref[...]
    y = x_ref[...] + rw[:, 0:1] * ybuf[slot, pl.ds(0, tc), :] + rw[:, 1:2] * ybuf[slot, pl.ds(tc, tc), :]
    if final_norm:
        ms = jnp.mean(y * y, axis=-1, keepdims=True)
        y = (y * lax.rsqrt(ms + RMS_EPS)) * g_ref[...]
    o_ref[...] = y


def _combine(x, route_w, pos, y_sorted, g_final, *, final_norm, tc=256):
    t, d = x.shape
    tc = min(tc, t)
    assert t % tc == 0
    nt = t // tc
    pos3 = pos.reshape(nt, tc, 2).transpose(0, 2, 1).reshape(nt, 1, 2 * tc)
    return pl.pallas_call(
        functools.partial(_combine_kernel, tc=tc, final_norm=final_norm),
        grid=(nt,),
        in_specs=[
            pl.BlockSpec((1, 1, 2 * tc), lambda i: (i, 0, 0), memory_space=pltpu.SMEM),
            pl.BlockSpec((1, 1, 2 * tc), lambda i: (jnp.minimum(i + 1, nt - 1), 0, 0),
                         memory_space=pltpu.SMEM),
            pl.BlockSpec((tc, d), lambda i: (i, 0)),
            pl.BlockSpec((tc, LANES), lambda i: (i, 0)),
            pl.BlockSpec(memory_space=pl.ANY),
            pl.BlockSpec((1, d), lambda i: (0, 0)),
        ],
        out_specs=pl.BlockSpec((tc, d), lambda i: (i, 0)),
        out_shape=jax.ShapeDtypeStruct((t, d), F32),
        scratch_shapes=[pltpu.VMEM((2, 2 * tc, d), F32), pltpu.SemaphoreType.DMA((2,))],
        compiler_params=_cparams(("arbitrary",)),
    )(pos3, pos3, x, route_w, y_sorted, g_final.reshape(1, d))


def _dispatch(route_i, blk):
    t = route_i.shape[0]
    eid = route_i[:, :2].reshape(-1)
    n_assign = eid.shape[0]
    onehot = (eid[:, None] == jnp.arange(MOE_N_EXPERTS, dtype=I32)[None, :]).astype(F32)
    tile = 256
    oh = onehot.reshape(n_assign // tile, tile, MOE_N_EXPERTS)
    within = jnp.einsum('ij,tje->tie', jnp.tril(jnp.ones((tile, tile), F32)), oh)
    tile_tot = within[:, -1, :]
    tile_off = jnp.cumsum(tile_tot, axis=0) - tile_tot
    csum = within + tile_off[:, None, :]
    rank = (jnp.sum(csum * oh, axis=2) - 1.0).reshape(n_assign).astype(I32)
    counts = (tile_off[-1] + tile_tot[-1]).astype(I32)
    padded = ((counts + blk - 1) // blk) * blk
    pend = jnp.cumsum(padded)
    pstart = pend - padded
    dest = (pstart[eid] + rank).astype(I32)
    n_rows = n_assign + MOE_N_EXPERTS * blk
    n_blocks = n_rows // blk
    row_tok = jnp.zeros((n_rows,), I32).at[dest].set(jnp.arange(n_assign, dtype=I32) // 2)
    blk_start = jnp.arange(n_blocks, dtype=I32) * blk
    blk_eid = jnp.minimum(jnp.sum((pend[None, :] <= blk_start[:, None]).astype(I32), axis=1),
                          MOE_N_EXPERTS - 1)
    return row_tok, blk_eid, dest.reshape(t, 2)


def _router_weights(w_group, w_expert):
    d = w_group.shape[0]
    w = jnp.zeros((d, LANES), F32)
    w = w.at[:, :MOE_GROUPS].set(w_group).at[:, MOE_GROUPS:MOE_GROUPS + MOE_N_EXPERTS].set(w_expert)
    return w.astype(BF16)


def _moe_layer(x1, route_i, route_w, g_ffn, w_gate, w_up, w_down, g_final, *, layer, final_norm,
               blk=256):
    row_tok, blk_eid, pos = _dispatch(route_i, blk)
    stack = lambda w: w.reshape((-1,) + w.shape[2:])
    y_sorted = _moe_ffn(x1, g_ffn, blk_eid, row_tok, stack(w_gate), stack(w_up),
                        stack(w_down), blk=blk, layer=layer)
    return _combine(x1, route_w, pos, y_sorted, g_final, final_norm=final_norm)


def kernel(x, norm_mix, norm_ffn, gla_w_in, gla_w_gate_up, gla_b_gate, gla_norm, gla_w_out,
           diff_w_in, diff_lambda_q1, diff_lambda_k1, diff_lambda_q2, diff_lambda_k2,
           diff_subln, diff_w_out, moe_w_group, moe_w_expert, moe_w_gate, moe_w_up,
           moe_w_down, final_norm):
    batch, seq, d = x.shape
    t = batch * seq
    xf = x.reshape(t, d)

    dk_total = d // 2
    n_main = 2 * dk_total + 2 * d
    w_in = gla_w_in[0]
    gla_dk = dk_total // GLA_HEADS
    colscale = jnp.ones((n_main,), F32).at[:dk_total].set(gla_dk ** -0.5)
    rank = w_in.shape[1] - n_main
    w_gz = jnp.zeros((d, LANES), F32).at[:, :rank].set(w_in[:, n_main:]).astype(BF16)
    proj, gz = _norm_matmul(xf, norm_mix[0], w_in[:, :n_main].astype(BF16), colscale, w_gz)
    o = _gla_scan(proj, gz, gla_w_gate_up[0], gla_b_gate[0], gla_norm[0], batch=batch, seq=seq)
    x1, ri, rw = _proj_router(o, gla_w_out[0].astype(BF16), xf, norm_ffn[0],
                              _router_weights(moe_w_group[0], moe_w_expert[0]))
    x2 = _moe_layer(x1, ri, rw, norm_ffn[0], moe_w_gate, moe_w_up, moe_w_down,
                    final_norm, layer=0, final_norm=False)

    hd = d // DIFF_HEADS // 2
    colscale = jnp.ones((3 * d,), F32).at[:d].set(hd ** -0.5 * LOG2E)
    qkv = _norm_matmul(x2, norm_mix[1], diff_w_in[0].astype(BF16), colscale)
    o = _diff_attention(qkv, diff_lambda_q1[0], diff_lambda_k1[0], diff_lambda_q2[0],
                        diff_lambda_k2[0], diff_subln[0], batch=batch, seq=seq, layer_idx=1)
    x3, ri, rw = _proj_router(o, diff_w_out[0].astype(BF16), x2, norm_ffn[1],
                              _router_weights(moe_w_group[1], moe_w_expert[1]))
    out = _moe_layer(x3, ri, rw, norm_ffn[1], moe_w_gate, moe_w_up, moe_w_down,
                     final_norm, layer=1, final_norm=True)
    return out.reshape(batch, seq, d)
```

```python
import functools
import math

import jax
import jax.numpy as jnp
from jax import lax
from jax.experimental import pallas as pl
from jax.experimental.pallas import tpu as pltpu

F32 = jnp.float32
BF16 = jnp.bfloat16
I32 = jnp.int32

RMS_EPS = 1e-6
CHUNK = 64
GLA_BLOCK = 256
GLA_HEADS = 4
GLA_GATE_TAU = 16.0
DIFF_HEADS = 8
MOE_GROUPS = 4
MOE_EXPERTS_PER_GROUP = 8
MOE_N_EXPERTS = MOE_GROUPS * MOE_EXPERTS_PER_GROUP

LANES = 128
SLAB = 256
VMEM_LIMIT = 56 * 1024 * 1024
LOG2E = 1.4426950408889634
NEG_BIG = -1e30
EXP_ZERO = 160.0
NORM_SLACK = 1.0 + 2.0 ** -10
SQ_SLACK = 1.0 + 2.0 ** -7

HIGHEST = lax.Precision.HIGHEST


def _cparams(sem):
    return pltpu.CompilerParams(dimension_semantics=sem, vmem_limit_bytes=VMEM_LIMIT)


def _norm_matmul_kernel(x_ref, g_ref, w_ref, cs_ref, *rest, n_slab, has_aux):
    if has_aux:
        w2_ref, o_ref, o2_ref, hn_ref = rest
    else:
        o_ref, hn_ref = rest

    @pl.when(pl.program_id(1) == 0)
    def _():
        x = x_ref[...]
        ms = jnp.mean(x * x, axis=-1, keepdims=True)
        hn = ((x * lax.rsqrt(ms + RMS_EPS)) * g_ref[...]).astype(BF16)
        hn_ref[...] = hn
        if has_aux:
            o2_ref[...] = jnp.dot(hn, w2_ref[...], preferred_element_type=F32)

    acc = jnp.dot(hn_ref[...], w_ref[...], preferred_element_type=F32) * cs_ref[...]
    for s in range(n_slab):
        o_ref[s] = acc[:, s * SLAB:(s + 1) * SLAB].astype(BF16)


def _norm_matmul(x, g, w, colscale, w_aux=None, *, tm=1024, tn=1024):
    t, d = x.shape
    n = w.shape[1]
    tm = min(tm, t)
    assert t % tm == 0 and n % tn == 0 and tn % SLAB == 0
    n_slab = tn // SLAB
    has_aux = w_aux is not None
    in_specs = [
        pl.BlockSpec((tm, d), lambda i, j: (i, 0)),
        pl.BlockSpec((1, d), lambda i, j: (0, 0)),
        pl.BlockSpec((d, tn), lambda i, j: (0, j)),
        pl.BlockSpec((1, tn), lambda i, j: (0, j)),
    ]
    out_shape = [jax.ShapeDtypeStruct((n // SLAB, t, SLAB), BF16)]
    out_specs = [pl.BlockSpec((n_slab, tm, SLAB), lambda i, j: (j, i, 0))]
    args = [x, g.reshape(1, d), w, colscale.reshape(1, n)]
    if has_aux:
        in_specs.append(pl.BlockSpec((d, LANES), lambda i, j: (0, 0)))
        out_shape.append(jax.ShapeDtypeStruct((t, LANES), F32))
        out_specs.append(pl.BlockSpec((tm, LANES), lambda i, j: (i, 0)))
        args.append(w_aux)
    res = pl.pallas_call(
        functools.partial(_norm_matmul_kernel, n_slab=n_slab, has_aux=has_aux),
        grid=(t // tm, n // tn),
        in_specs=in_specs,
        out_specs=out_specs,
        out_shape=out_shape,
        scratch_shapes=[pltpu.VMEM((tm, d), BF16)],
        compiler_params=_cparams(("parallel", "arbitrary")),
    )(*args)
    return res if has_aux else res[0]


def _gla_kernel(q_ref, k_ref, v_ref, r_ref, gz_ref, wg_ref, bg_ref, ng_ref, o_ref,
                state_ref, la_ref, ob_ref, *, n_blocks):
    @pl.when(pl.program_id(2) == 0)
    def _():
        state_ref[...] = jnp.zeros_like(state_ref)

    def split3(x):
        hi = x.astype(BF16)
        r1 = x - hi.astype(F32)
        mid = r1.astype(BF16)
        return hi, mid, (r1 - mid.astype(F32)).astype(BF16)

    def dot32(a, b):
        return jnp.dot(a, b, preferred_element_type=F32)

    gh, gm, _ = split3(gz_ref[...])
    wh, wm, _ = split3(wg_ref[...])
    z = dot32(gh, wh) + dot32(gm, wh) + dot32(gh, wm) + bg_ref[...]
    la_ref[...] = -(jnp.maximum(-z, 0.0) + jnp.log1p(jnp.exp(-jnp.abs(z)))) * (1.0 / GLA_GATE_TAU)

    nb = GLA_BLOCK
    nc = nb // CHUNK
    row = lax.broadcasted_iota(I32, (nb, nb), 0)
    col = lax.broadcasted_iota(I32, (nb, nb), 1)
    same_chunk = (row // CHUNK) == (col // CHUNK)
    later = (same_chunk & (col > row)).astype(BF16)
    visible = (row // CHUNK) >= (col // CHUNK)
    dk = state_ref.shape[0]
    dv = state_ref.shape[1]
    nt = (((1,), (1,)), ((), ()))

    def per_chunk_rows(vals):
        return jnp.concatenate([jnp.broadcast_to(x, (CHUNK, dk)) for x in vals], axis=0)

    def body(i, carry):
        sl = pl.ds(pl.multiple_of(i * nb, nb), nb)
        la = la_ref[sl, :]
        suf = sum(dot32(later, t) for t in split3(la))
        tot = [suf[c * CHUNK:c * CHUNK + 1] + la[c * CHUNK:c * CHUNK + 1] for c in range(nc)]
        a = [tot[0]]
        for c in range(1, nc):
            a.append(a[-1] + tot[c])
        qf = q_ref[0, sl, :].astype(F32)
        k_dec = k_ref[0, sl, :].astype(F32) * jnp.exp(suf)
        v = jnp.concatenate([v_ref[0, sl, :], v_ref[1, sl, :]], axis=-1)
        s0 = state_ref[...]

        p = None
        zeros = jnp.zeros((CHUNK, dk), BF16)
        k_dec16 = k_dec.astype(BF16)
        for i_c in range(nc):
            q_i = (qf * per_chunk_rows([jnp.exp(jnp.minimum(a[c] - a[i_c], 0.0))
                                        for c in range(nc)])).astype(BF16)
            k_i = jnp.concatenate([k_dec16[c * CHUNK:(c + 1) * CHUNK] if c == i_c else zeros
                                   for c in range(nc)], axis=0)
            term = lax.dot_general(q_i, k_i, nt, preferred_element_type=F32)
            p = term if p is None else p + term
        p = jnp.where(visible, p, 0.0).astype(BF16)
        q_in = (qf * per_chunk_rows([jnp.exp(a[c]) for c in range(nc)])).astype(BF16)
        ob_ref[sl, :] = dot32(q_in, s0.astype(BF16)) + dot32(p, v)

        k_out = (k_dec * per_chunk_rows([jnp.exp(a[nc - 1] - a[c]) for c in range(nc)])).astype(BF16)
        upd = lax.dot_general(k_out, v, (((0,), (0,)), ((), ())), preferred_element_type=F32)
        decay = jnp.exp(jnp.broadcast_to(a[nc - 1], (LANES, dk)).T)
        state_ref[...] = jnp.concatenate([decay] * (dv // LANES), axis=-1) * s0 + upd
        return carry

    lax.fori_loop(0, n_blocks, body, 0)

    o = ob_ref[...]
    ms = jnp.mean(o * o, axis=-1, keepdims=True)
    on = (o * lax.rsqrt(ms + RMS_EPS)) * ng_ref[...]
    r = jnp.concatenate([r_ref[0], r_ref[1]], axis=-1).astype(F32)
    o_ref[...] = (on * (r * jax.nn.sigmoid(r))).astype(BF16)


def _gla_scan(proj, gz, w_gate_up, b_gate, norm_g, *, batch, seq, lc=1024):
    t = batch * seq
    lc = min(lc, seq)
    assert seq % lc == 0 and lc % GLA_BLOCK == 0 and GLA_BLOCK % CHUNK == 0
    ncb = seq // lc
    dk = SLAB
    dv = 2 * SLAB
    d_out = GLA_HEADS * dv
    rank = w_gate_up.shape[0]
    wg = jnp.zeros((LANES, GLA_HEADS * dk), F32).at[:rank].set(w_gate_up)

    def tok(b, h, c):
        return b * ncb + c

    return pl.pallas_call(
        functools.partial(_gla_kernel, n_blocks=lc // GLA_BLOCK),
        grid=(batch, GLA_HEADS, ncb),
        in_specs=[
            pl.BlockSpec((1, lc, SLAB), lambda b, h, c: (h, tok(b, h, c), 0)),
            pl.BlockSpec((1, lc, SLAB), lambda b, h, c: (GLA_HEADS + h, tok(b, h, c), 0)),
            pl.BlockSpec((2, lc, SLAB), lambda b, h, c: (GLA_HEADS + h, tok(b, h, c), 0)),
            pl.BlockSpec((2, lc, SLAB), lambda b, h, c: (2 * GLA_HEADS + h, tok(b, h, c), 0)),
            pl.BlockSpec((lc, LANES), lambda b, h, c: (tok(b, h, c), 0)),
            pl.BlockSpec((LANES, dk), lambda b, h, c: (0, h)),
            pl.BlockSpec((1, dk), lambda b, h, c: (0, h)),
            pl.BlockSpec((1, dv), lambda b, h, c: (0, 0)),
        ],
        out_specs=pl.BlockSpec((lc, dv), lambda b, h, c: (tok(b, h, c), h)),
        out_shape=jax.ShapeDtypeStruct((t, d_out), BF16),
        scratch_shapes=[pltpu.VMEM((dk, dv), F32), pltpu.VMEM((lc, dk), F32),
                        pltpu.VMEM((lc, dv), F32)],
        compiler_params=_cparams(("parallel", "parallel", "arbitrary")),
    )(proj, proj, proj, proj, gz, wg, b_gate.reshape(1, -1), norm_g.reshape(1, -1))


def _diff_attn_kernel(slope_ref, invct_ref, q_ref, k_ref, v_ref, qfeat_ref, kfeat_ref, lq1_ref, lk1_ref,
                      lq2_ref, lk2_ref, sg_ref, o_ref, vt_ref, dbias_ref, r0_ref, kmax_ref, m_ref, l_ref, acc_ref,
                      *, tq, lambda_init):
    h = pl.program_id(1)
    qi = pl.program_id(2)
    hd = SLAB // 2
    c = slope_ref[h] * LOG2E
    nt = (((1,), (1,)), ((), ()))

    ones16 = jnp.ones((16, hd), BF16)

    def sq_norms(x):
        xf = x.astype(F32)
        sq = (xf * xf * SQ_SLACK).astype(BF16)
        return lax.dot_general(ones16, sq, nt, preferred_element_type=F32)[0:1]

    @pl.when(qi == 0)
    def _():
        def per_block(i, carry):
            sl = pl.ds(pl.multiple_of(i * tq, tq), tq)
            vt_ref[:, sl] = v_ref[0, sl, :].T
            kb = k_ref[0, sl, :]
            return tuple(jnp.maximum(carry[u], sq_norms(kb[:, u * hd:(u + 1) * hd]))
                         for u in range(2))
        zero = jnp.zeros((1, tq), F32)
        kn2 = lax.fori_loop(0, v_ref.shape[1] // tq, per_block, (zero, zero))
        for u in range(2):
            kmax_ref[u] = jnp.broadcast_to(jnp.sqrt(jnp.max(kn2[u], axis=1, keepdims=True)),
                                           (8, LANES))
        ki = lax.broadcasted_iota(I32, (tq, tq), 0)
        qj = lax.broadcasted_iota(I32, (tq, tq), 1)
        bias = c * (qj - jnp.abs(qj - ki)).astype(F32)
        dbias_ref[...] = jnp.where((qj // CHUNK) >= (ki // CHUNK), bias, -jnp.inf)

    q = q_ref[0]
    start = pl.multiple_of(qi * tq, tq)

    k = k_ref[0, pl.ds(start, tq), :]
    vt = vt_ref[:, pl.ds(start, tq)]
    for u in range(2):
        r = lax.dot_general(k[:, u * hd:(u + 1) * hd], q[:, u * hd:(u + 1) * hd], nt,
                            preferred_element_type=F32) + dbias_ref[...]
        m_new = jnp.max(r, axis=0, keepdims=True)
        p = jnp.exp2(r - m_new)
        l_ref[u] = jnp.sum(p, axis=0, keepdims=True)
        acc_ref[u] = jnp.dot(vt, p.astype(BF16), preferred_element_type=F32)
        m_ref[u] = m_new

    qfeat = jnp.broadcast_to(qfeat_ref[0], (tq, LANES))
    kfeat = kfeat_ref[...]
    qa = [jnp.concatenate([q[:, u * hd:(u + 1) * hd], qfeat], axis=-1) for u in range(2)]

    def scores(u, blk):
        st = pl.multiple_of(blk * tq, tq)
        ka = jnp.concatenate([k_ref[0, pl.ds(st, tq), u * hd:(u + 1) * hd], kfeat], axis=-1)
        return lax.dot_general(ka, qa[u], nt, preferred_element_type=F32)

    def accumulate(u, r, vt, off):
        m_prev = m_ref[u]
        m_new = jnp.maximum(m_prev, jnp.max(r, axis=0, keepdims=True) - off)
        alpha = jnp.exp2(m_prev - m_new)
        p = jnp.exp2(r - (m_new + off))
        l_ref[u] = alpha * l_ref[u] + jnp.sum(p, axis=0, keepdims=True)
        acc_ref[u] = alpha * acc_ref[u] + jnp.dot(vt, p.astype(BF16),
                                                  preferred_element_type=F32)
        m_ref[u] = m_new

    r0_ref[...] = scores(0, jnp.maximum(qi - 1, 0))

    def past_block(dd, carry):
        blk = qi - dd
        vt = vt_ref[:, pl.ds(pl.multiple_of(blk * tq, tq), tq)]
        off = c * (dd * tq).astype(F32)
        r1 = scores(1, blk)
        accumulate(0, r0_ref[...], vt, off)
        r0_ref[...] = scores(0, jnp.maximum(blk - 1, 0))
        accumulate(1, r1, vt, off)
        return carry

    gap = None
    for u in range(2):
        bound = (jnp.sqrt(sq_norms(q[:, u * hd:(u + 1) * hd])) * kmax_ref[u][0:1, 0:1]
                 * NORM_SLACK - m_ref[u])
        g = jnp.max(bound, axis=1, keepdims=True)
        gap = g if gap is None else jnp.maximum(gap, g)
    n_need = jnp.ceil((gap + EXP_ZERO) * invct_ref[h])
    n_need = jnp.clip(n_need, 0.0, float(v_ref.shape[1] // tq)).astype(I32)[0, 0]
    n_past = jnp.minimum(qi, n_need)

    def past_pair(i, carry):
        past_block(2 * i + 1, carry)
        return past_block(2 * i + 2, carry)

    lax.fori_loop(0, n_past // 2, past_pair, 0)

    @pl.when(n_past % 2 == 1)
    def _():
        past_block(n_past, 0)

    lam = (jnp.exp(jnp.sum(lq1_ref[...] * lk1_ref[...], axis=-1, keepdims=True))
           - jnp.exp(jnp.sum(lq2_ref[...] * lk2_ref[...], axis=-1, keepdims=True))
           + lambda_init)
    o = acc_ref[0] * (1.0 / l_ref[0]) - lam * (acc_ref[1] * (1.0 / l_ref[1]))
    o = o.T
    ms = jnp.mean(o * o, axis=-1, keepdims=True)
    o = (o * lax.rsqrt(ms + RMS_EPS)) * sg_ref[...]
    o_ref[...] = (o * (1.0 - lambda_init)).astype(BF16)


def _alibi_features(slopes, tq):
    c = slopes * LOG2E
    c_hi = c.astype(BF16).astype(F32)
    c_lo = c - c_hi
    qf = jnp.zeros((slopes.shape[0], 1, LANES), F32)
    qf = qf.at[:, 0, 0].set(LANES * c_hi).at[:, 0, 1].set(c_hi)
    qf = qf.at[:, 0, 2].set(LANES * c_lo).at[:, 0, 3].set(c_lo)
    idx = jnp.arange(tq, dtype=I32)
    hi = (idx // LANES).astype(F32)
    lo = (idx % LANES).astype(F32)
    kf = jnp.zeros((tq, LANES), F32).at[:, 0].set(hi).at[:, 1].set(lo).at[:, 2].set(hi).at[:, 3].set(lo)
    return qf.astype(BF16), kf.astype(BF16)


def _diff_attention(qkv, lq1, lk1, lq2, lk2, subln, *, batch, seq, layer_idx, tq=512):
    t = batch * seq
    tq = min(tq, seq)
    assert seq % tq == 0 and tq % CHUNK == 0
    nq = seq // tq
    lambda_init = 0.8 - 0.6 * math.exp(-0.3 * layer_idx)
    slopes = 2.0 ** (-8.0 * jnp.arange(1, DIFF_HEADS + 1, dtype=F32) / DIFF_HEADS)
    vec = lambda a: a.reshape(1, -1).astype(F32)
    qfeat, kfeat = _alibi_features(slopes, tq)
    grid_spec = pltpu.PrefetchScalarGridSpec(
        num_scalar_prefetch=2,
        grid=(batch, DIFF_HEADS, nq),
        in_specs=[
            pl.BlockSpec((1, tq, SLAB), lambda b, h, i, s, n: (h, b * nq + i, 0)),
            pl.BlockSpec((1, seq, SLAB), lambda b, h, i, s, n: (DIFF_HEADS + h, b, 0)),
            pl.BlockSpec((1, seq, SLAB), lambda b, h, i, s, n: (2 * DIFF_HEADS + h, b, 0)),
            pl.BlockSpec((1, 1, LANES), lambda b, h, i, s, n: (h, 0, 0)),
            pl.BlockSpec((tq, LANES), lambda b, h, i, s, n: (0, 0)),
            pl.BlockSpec((1, SLAB // 2), lambda b, h, i, s, n: (0, 0)),
            pl.BlockSpec((1, SLAB // 2), lambda b, h, i, s, n: (0, 0)),
            pl.BlockSpec((1, SLAB // 2), lambda b, h, i, s, n: (0, 0)),
            pl.BlockSpec((1, SLAB // 2), lambda b, h, i, s, n: (0, 0)),
            pl.BlockSpec((1, SLAB), lambda b, h, i, s, n: (0, 0)),
        ],
        out_specs=pl.BlockSpec((tq, SLAB), lambda b, h, i, s, n: (b * nq + i, h)),
        scratch_shapes=[pltpu.VMEM((SLAB, seq), BF16), pltpu.VMEM((tq, tq), F32),
                        pltpu.VMEM((tq, tq), F32), pltpu.VMEM((2, 8, LANES), F32),
                        pltpu.VMEM((2, 1, tq), F32), pltpu.VMEM((2, 1, tq), F32),
                        pltpu.VMEM((2, SLAB, tq), F32)],
    )
    return pl.pallas_call(
        functools.partial(_diff_attn_kernel, tq=tq, lambda_init=lambda_init),
        grid_spec=grid_spec,
        out_shape=jax.ShapeDtypeStruct((t, DIFF_HEADS * SLAB), BF16),
        compiler_params=_cparams(("parallel", "parallel", "arbitrary")),
    )(slopes, 1.0 / (slopes * (LOG2E * tq)), qkv, qkv, qkv, qfeat, kfeat, vec(lq1), vec(lk1), vec(lq2), vec(lk2), vec(subln))


def _proj_router_kernel(o_ref, w_ref, x_ref, g_ref, wr_ref, x1_ref, ri_ref, rw_ref):
    x1 = x_ref[...] + jnp.dot(o_ref[...], w_ref[...], preferred_element_type=F32)
    x1_ref[...] = x1
    ms = jnp.mean(x1 * x1, axis=-1, keepdims=True)
    hn = ((x1 * lax.rsqrt(ms + RMS_EPS)) * g_ref[...]).astype(BF16)
    logits = jnp.dot(hn, wr_ref[...], preferred_element_type=F32)
    lane = lax.broadcasted_iota(I32, logits.shape, 1)
    neg = -jnp.inf

    def first_max(vals):
        m = jnp.max(vals, axis=-1, keepdims=True)
        idx = jnp.min(jnp.where(vals == m, lane, LANES), axis=-1, keepdims=True)
        return m, idx

    gmask = lane < MOE_GROUPS
    gmax, gidx = first_max(jnp.where(gmask, logits, neg))
    gden = jnp.sum(jnp.where(gmask, jnp.exp(logits - gmax), 0.0), axis=-1, keepdims=True)
    g_w = 1.0 / gden
    lo = MOE_GROUPS + MOE_EXPERTS_PER_GROUP * gidx
    emask = (lane >= lo) & (lane < lo + MOE_EXPERTS_PER_GROUP)
    elog = jnp.where(emask, logits, neg)
    m1, i1 = first_max(elog)
    m2, i2 = first_max(jnp.where(lane == i1, neg, elog))
    tt = jnp.exp(m2 - m1)
    w1 = g_w / (1.0 + tt)
    w2 = w1 * tt
    ri_ref[...] = jnp.where(lane == 0, i1 - MOE_GROUPS, jnp.where(lane == 1, i2 - MOE_GROUPS, 0))
    rw_ref[...] = jnp.where(lane == 0, w1, jnp.where(lane == 1, w2, 0.0))


def _proj_router(o, w_out, x, g_ffn, w_router, *, tm=512):
    t, d = x.shape
    tm = min(tm, t)
    assert t % tm == 0
    return pl.pallas_call(
        _proj_router_kernel,
        grid=(t // tm,),
        in_specs=[
            pl.BlockSpec((tm, d), lambda i: (i, 0)),
            pl.BlockSpec((d, d), lambda i: (0, 0)),
            pl.BlockSpec((tm, d), lambda i: (i, 0)),
            pl.BlockSpec((1, d), lambda i: (0, 0)),
            pl.BlockSpec((d, LANES), lambda i: (0, 0)),
        ],
        out_specs=[
            pl.BlockSpec((tm, d), lambda i: (i, 0)),
            pl.BlockSpec((tm, LANES), lambda i: (i, 0)),
            pl.BlockSpec((tm, LANES), lambda i: (i, 0)),
        ],
        out_shape=[jax.ShapeDtypeStruct((t, d), F32), jax.ShapeDtypeStruct((t, LANES), I32),
                   jax.ShapeDtypeStruct((t, LANES), F32)],
        compiler_params=_cparams(("parallel",)),
    )(o, w_out, x, g_ffn.reshape(1, d), w_router)


def _token_copy(src_hbm, dst_ref, sem, src_tok, dst_tok, rt):
    return pltpu.make_async_copy(src_hbm.at[pl.ds(pl.multiple_of(src_tok * rt, rt), rt), :],
                                 dst_ref.at[pl.ds(dst_tok * rt, rt), :], sem)


def _gather_rows(idx_ref, n, src_hbm, dst_ref, sem, rt=1):
    def body(r, carry):
        _token_copy(src_hbm, dst_ref, sem, idx_ref[0, 0, r], r, rt).start()
        return carry
    lax.fori_loop(0, n, body, 0, unroll=8)


def _wait_rows(n, src_hbm, dst_ref, sem, rt=1):
    pltpu.make_async_copy(src_hbm.at[pl.ds(0, n * rt), :], dst_ref, sem).wait()


def _moe_kernel(eid_ref, tok_ref, tok_next_ref, x_hbm, g_ref, wg32_ref, wu32_ref,
                wd32_ref, o_ref, xbuf, sem, wg_ref, wu_ref, wd_ref, *, blk):
    b = pl.program_id(0)
    nb = pl.num_programs(0)
    slot = b % 2

    @pl.when((b == 0) | (eid_ref[b] != eid_ref[jnp.maximum(b - 1, 0)]))
    def _():
        wg_ref[0] = wg32_ref[0].astype(BF16)
        wu_ref[0] = wu32_ref[0].astype(BF16)
        wd_ref[0] = wd32_ref[0].astype(BF16)

    @pl.when(b == 0)
    def _():
        _gather_rows(tok_ref, blk, x_hbm, xbuf.at[0], sem.at[0])

    _wait_rows(blk, x_hbm, xbuf.at[slot], sem.at[slot])

    x = xbuf[slot]
    ms = jnp.mean(x * x, axis=-1, keepdims=True)
    hn = ((x * lax.rsqrt(ms + RMS_EPS)) * g_ref[...]).astype(BF16)

    def request(quarter):
        for r in range(quarter * blk // 4, (quarter + 1) * blk // 4):
            _token_copy(x_hbm, xbuf.at[1 - slot], sem.at[1 - slot], tok_next_ref[0, 0, r], r,
                        1).start()

    request(0)
    hg = jnp.dot(hn, wg_ref[0], preferred_element_type=F32)
    request(1)
    hu = jnp.dot(hn, wu_ref[0], preferred_element_type=F32)
    request(2)
    act = ((hg * jax.nn.sigmoid(hg)) * hu).astype(BF16)
    out = jnp.dot(act, wd_ref[0], preferred_element_type=F32)
    request(3)
    o_ref[...] = out

    @pl.when(b == nb - 1)
    def _():
        _wait_rows(blk, x_hbm, xbuf.at[1 - slot], sem.at[1 - slot])


def _moe_ffn(x, g_ffn, blk_eid, row_tok, w_gate, w_up, w_down, *, blk, layer):
    t, d = x.shape
    n_rows = row_tok.shape[0]
    n_blocks = n_rows // blk
    dff = w_gate.shape[-1]
    first = layer * MOE_N_EXPERTS
    tok3 = row_tok.reshape(n_blocks, 1, blk)
    grid_spec = pltpu.PrefetchScalarGridSpec(
        num_scalar_prefetch=1,
        grid=(n_blocks,),
        in_specs=[
            pl.BlockSpec((1, 1, blk), lambda b, e: (b, 0, 0), memory_space=pltpu.SMEM),
            pl.BlockSpec((1, 1, blk), lambda b, e: (jnp.minimum(b + 1, n_blocks - 1), 0, 0),
                         memory_space=pltpu.SMEM),
            pl.BlockSpec(memory_space=pl.ANY),
            pl.BlockSpec((1, d), lambda b, e: (0, 0)),
            pl.BlockSpec((1, d, dff), lambda b, e: (first + e[b], 0, 0)),
            pl.BlockSpec((1, d, dff), lambda b, e: (first + e[b], 0, 0)),
            pl.BlockSpec((1, dff, d), lambda b, e: (first + e[b], 0, 0)),
        ],
        out_specs=pl.BlockSpec((blk, d), lambda b, e: (b, 0)),
        scratch_shapes=[pltpu.VMEM((2, blk, d), F32), pltpu.SemaphoreType.DMA((2,)),
                        pltpu.VMEM((1, d, dff), BF16), pltpu.VMEM((1, d, dff), BF16),
                        pltpu.VMEM((1, dff, d), BF16)],
    )
    return pl.pallas_call(
        functools.partial(_moe_kernel, blk=blk),
        grid_spec=grid_spec,
        out_shape=jax.ShapeDtypeStruct((n_rows, d), F32),
        compiler_params=_cparams(("arbitrary",)),
    )(blk_eid, tok3, tok3, x, g_ffn.reshape(1, d), w_gate, w_up, w_down)


def _combine_kernel(pos_ref, pos_next_ref, x_ref, rw_ref, y_hbm, g_ref, o_ref, ybuf, sem,
                    *, tc, final_norm):
    i = pl.program_id(0)
    n = pl.num_programs(0)
    slot = i % 2

    @pl.when(i == 0)
    def _():
        _gather_rows(pos_ref, 2 * tc, y_hbm, ybuf.at[0], sem.at[0])

    @pl.when(i + 1 < n)
    def _():
        _gather_rows(pos_next_ref, 2 * tc, y_hbm, ybuf.at[1 - slot], sem.at[1 - slot])

    _wait_rows(2 * tc, y_hbm, ybuf.at[slot], sem.at[slot])

    rw = rw_ref[...]
    y = x_ref[...] + rw[:, 0:1] * ybuf[slot, pl.ds(0, tc), :] + rw[:, 1:2] * ybuf[slot, pl.ds(tc, tc), :]
    if final_norm:
        ms = jnp.mean(y * y, axis=-1, keepdims=True)
        y = (y * lax.rsqrt(ms + RMS_EPS)) * g_ref[...]
    o_ref[...] = y


def _combine(x, route_w, pos, y_sorted, g_final, *, final_norm, tc=256):
    t, d = x.shape
    tc = min(tc, t)
    assert t % tc == 0
    nt = t // tc
    pos3 = pos.reshape(nt, tc, 2).transpose(0, 2, 1).reshape(nt, 1, 2 * tc)
    return pl.pallas_call(
        functools.partial(_combine_kernel, tc=tc, final_norm=final_norm),
        grid=(nt,),
        in_specs=[
            pl.BlockSpec((1, 1, 2 * tc), lambda i: (i, 0, 0), memory_space=pltpu.SMEM),
            pl.BlockSpec((1, 1, 2 * tc), lambda i: (jnp.minimum(i + 1, nt - 1), 0, 0),
                         memory_space=pltpu.SMEM),
            pl.BlockSpec((tc, d), lambda i: (i, 0)),
            pl.BlockSpec((tc, LANES), lambda i: (i, 0)),
            pl.BlockSpec(memory_space=pl.ANY),
            pl.BlockSpec((1, d), lambda i: (0, 0)),
        ],
        out_specs=pl.BlockSpec((tc, d), lambda i: (i, 0)),
        out_shape=jax.ShapeDtypeStruct((t, d), F32),
        scratch_shapes=[pltpu.VMEM((2, 2 * tc, d), F32), pltpu.SemaphoreType.DMA((2,))],
        compiler_params=_cparams(("arbitrary",)),
    )(pos3, pos3, x, route_w, y_sorted, g_final.reshape(1, d))


def _dispatch(route_i, blk):
    t = route_i.shape[0]
    eid = route_i[:, :2].reshape(-1)
    n_assign = eid.shape[0]
    onehot = (eid[:, None] == jnp.arange(MOE_N_EXPERTS, dtype=I32)[None, :]).astype(F32)
    tile = 256
    oh = onehot.reshape(n_assign // tile, tile, MOE_N_EXPERTS)
    within = jnp.einsum('ij,tje->tie', jnp.tril(jnp.ones((tile, tile), F32)), oh)
    tile_tot = within[:, -1, :]
    tile_off = jnp.cumsum(tile_tot, axis=0) - tile_tot
    csum = within + tile_off[:, None, :]
    rank = (jnp.sum(csum * oh, axis=2) - 1.0).reshape(n_assign).astype(I32)
    counts = (tile_off[-1] + tile_tot[-1]).astype(I32)
    padded = ((counts + blk - 1) // blk) * blk
    pend = jnp.cumsum(padded)
    pstart = pend - padded
    dest = (pstart[eid] + rank).astype(I32)
    n_rows = n_assign + MOE_N_EXPERTS * blk
    n_blocks = n_rows // blk
    row_tok = jnp.zeros((n_rows,), I32).at[dest].set(jnp.arange(n_assign, dtype=I32) // 2)
    blk_start = jnp.arange(n_blocks, dtype=I32) * blk
    blk_eid = jnp.minimum(jnp.sum((pend[None, :] <= blk_start[:, None]).astype(I32), axis=1),
                          MOE_N_EXPERTS - 1)
    return row_tok, blk_eid, dest.reshape(t, 2)


def _router_weights(w_group, w_expert):
    d = w_group.shape[0]
    w = jnp.zeros((d, LANES), F32)
    w = w.at[:, :MOE_GROUPS].set(w_group).at[:, MOE_GROUPS:MOE_GROUPS + MOE_N_EXPERTS].set(w_expert)
    return w.astype(BF16)


def _moe_layer(x1, route_i, route_w, g_ffn, w_gate, w_up, w_down, g_final, *, layer, final_norm,
               blk=256):
    row_tok, blk_eid, pos = _dispatch(route_i, blk)
    stack = lambda w: w.reshape((-1,) + w.shape[2:])
    y_sorted = _moe_ffn(x1, g_ffn, blk_eid, row_tok, stack(w_gate), stack(w_up),
                        stack(w_down), blk=blk, layer=layer)
    return _combine(x1, route_w, pos, y_sorted, g_final, final_norm=final_norm)


def kernel(x, norm_mix, norm_ffn, gla_w_in, gla_w_gate_up, gla_b_gate, gla_norm, gla_w_out,
           diff_w_in, diff_lambda_q1, diff_lambda_k1, diff_lambda_q2, diff_lambda_k2,
           diff_subln, diff_w_out, moe_w_group, moe_w_expert, moe_w_gate, moe_w_up,
           moe_w_down, final_norm):
    batch, seq, d = x.shape
    t = batch * seq
    xf = x.reshape(t, d)

    dk_total = d // 2
    n_main = 2 * dk_total + 2 * d
    w_in = gla_w_in[0]
    gla_dk = dk_total // GLA_HEADS
    colscale = jnp.ones((n_main,), F32).at[:dk_total].set(gla_dk ** -0.5)
    rank = w_in.shape[1] - n_main
    w_gz = jnp.zeros((d, LANES), F32).at[:, :rank].set(w_in[:, n_main:]).astype(BF16)
    proj, gz = _norm_matmul(xf, norm_mix[0], w_in[:, :n_main].astype(BF16), colscale, w_gz)
    o = _gla_scan(proj, gz, gla_w_gate_up[0], gla_b_gate[0], gla_norm[0], batch=batch, seq=seq)
    x1, ri, rw = _proj_router(o, gla_w_out[0].astype(BF16), xf, norm_ffn[0],
                              _router_weights(moe_w_group[0], moe_w_expert[0]))
    x2 = _moe_layer(x1, ri, rw, norm_ffn[0], moe_w_gate, moe_w_up, moe_w_down,
                    final_norm, layer=0, final_norm=False)

    hd = d // DIFF_HEADS // 2
    colscale = jnp.ones((3 * d,), F32).at[:d].set(hd ** -0.5 * LOG2E)
    qkv = _norm_matmul(x2, norm_mix[1], diff_w_in[0].astype(BF16), colscale)
    o = _diff_attention(qkv, diff_lambda_q1[0], diff_lambda_k1[0], diff_lambda_q2[0],
                        diff_lambda_k2[0], diff_subln[0], batch=batch, seq=seq, layer_idx=1)
    x3, ri, rw = _proj_router(o, diff_w_out[0].astype(BF16), x2, norm_ffn[1],
                              _router_weights(moe_w_group[1], moe_w_expert[1]))
    out = _moe_layer(x3, ri, rw, norm_ffn[1], moe_w_gate, moe_w_up, moe_w_down,
                     final_norm, layer=1, final_norm=True)
    return out.reshape(batch, seq, d)
```

```python
import functools
import math

import jax
import jax.numpy as jnp
from jax import lax
from jax.experimental import pallas as pl
from jax.experimental.pallas import tpu as pltpu

F32 = jnp.float32
BF16 = jnp.bfloat16
I32 = jnp.int32

RMS_EPS = 1e-6
CHUNK = 64
GLA_BLOCK = 256
GLA_HEADS = 4
GLA_GATE_TAU = 16.0
DIFF_HEADS = 8
MOE_GROUPS = 4
MOE_EXPERTS_PER_GROUP = 8
MOE_N_EXPERTS = MOE_GROUPS * MOE_EXPERTS_PER_GROUP

LANES = 128
SLAB = 256
VMEM_LIMIT = 56 * 1024 * 1024
LOG2E = 1.4426950408889634
NEG_BIG = -1e30
EXP_ZERO = 160.0
NORM_SLACK = 1.0 + 2.0 ** -10
SQ_SLACK = 1.0 + 2.0 ** -7

HIGHEST = lax.Precision.HIGHEST


def _cparams(sem):
    return pltpu.CompilerParams(dimension_semantics=sem, vmem_limit_bytes=VMEM_LIMIT)


def _norm_matmul_kernel(x_ref, g_ref, w_ref, cs_ref, *rest, n_slab, has_aux):
    if has_aux:
        w2_ref, o_ref, o2_ref, hn_ref = rest
    else:
        o_ref, hn_ref = rest

    @pl.when(pl.program_id(1) == 0)
    def _():
        x = x_ref[...]
        ms = jnp.mean(x * x, axis=-1, keepdims=True)
        hn = ((x * lax.rsqrt(ms + RMS_EPS)) * g_ref[...]).astype(BF16)
        hn_ref[...] = hn
        if has_aux:
            o2_ref[...] = jnp.dot(hn, w2_ref[...], preferred_element_type=F32)

    acc = jnp.dot(hn_ref[...], w_ref[...], preferred_element_type=F32) * cs_ref[...]
    for s in range(n_slab):
        o_ref[s] = acc[:, s * SLAB:(s + 1) * SLAB].astype(BF16)


def _norm_matmul(x, g, w, colscale, w_aux=None, *, tm=1024, tn=1024):
    t, d = x.shape
    n = w.shape[1]
    tm = min(tm, t)
    assert t % tm == 0 and n % tn == 0 and tn % SLAB == 0
    n_slab = tn // SLAB
    has_aux = w_aux is not None
    in_specs = [
        pl.BlockSpec((tm, d), lambda i, j: (i, 0)),
        pl.BlockSpec((1, d), lambda i, j: (0, 0)),
        pl.BlockSpec((d, tn), lambda i, j: (0, j)),
        pl.BlockSpec((1, tn), lambda i, j: (0, j)),
    ]
    out_shape = [jax.ShapeDtypeStruct((n // SLAB, t, SLAB), BF16)]
    out_specs = [pl.BlockSpec((n_slab, tm, SLAB), lambda i, j: (j, i, 0))]
    args = [x, g.reshape(1, d), w, colscale.reshape(1, n)]
    if has_aux:
        in_specs.append(pl.BlockSpec((d, LANES), lambda i, j: (0, 0)))
        out_shape.append(jax.ShapeDtypeStruct((t, LANES), F32))
        out_specs.append(pl.BlockSpec((tm, LANES), lambda i, j: (i, 0)))
        args.append(w_aux)
    res = pl.pallas_call(
        functools.partial(_norm_matmul_kernel, n_slab=n_slab, has_aux=has_aux),
        grid=(t // tm, n // tn),
        in_specs=in_specs,
        out_specs=out_specs,
        out_shape=out_shape,
        scratch_shapes=[pltpu.VMEM((tm, d), BF16)],
        compiler_params=_cparams(("parallel", "arbitrary")),
    )(*args)
    return res if has_aux else res[0]


def _gla_kernel(q_ref, k_ref, v_ref, r_ref, gz_ref, wg_ref, bg_ref, ng_ref, o_ref,
                state_ref, la_ref, ob_ref, *, n_blocks):
    @pl.when(pl.program_id(2) == 0)
    def _():
        state_ref[...] = jnp.zeros_like(state_ref)

    def split3(x):
        hi = x.astype(BF16)
        r1 = x - hi.astype(F32)
        mid = r1.astype(BF16)
        return hi, mid, (r1 - mid.astype(F32)).astype(BF16)

    def dot32(a, b):
        return jnp.dot(a, b, preferred_element_type=F32)

    gh, gm, _ = split3(gz_ref[...])
    wh, wm, _ = split3(wg_ref[...])
    z = dot32(gh, wh) + dot32(gm, wh) + dot32(gh, wm) + bg_ref[...]
    la_ref[...] = -(jnp.maximum(-z, 0.0) + jnp.log1p(jnp.exp(-jnp.abs(z)))) * (1.0 / GLA_GATE_TAU)

    nb = GLA_BLOCK
    nc = nb // CHUNK
    row = lax.broadcasted_iota(I32, (nb, nb), 0)
    col = lax.broadcasted_iota(I32, (nb, nb), 1)
    same_chunk = (row // CHUNK) == (col // CHUNK)
    later = (same_chunk & (col > row)).astype(BF16)
    visible = (row // CHUNK) >= (col // CHUNK)
    dk = state_ref.shape[0]
    dv = state_ref.shape[1]
    nt = (((1,), (1,)), ((), ()))

    def per_chunk_rows(vals):
        return jnp.concatenate([jnp.broadcast_to(x, (CHUNK, dk)) for x in vals], axis=0)

    def body(i, carry):
        sl = pl.ds(pl.multiple_of(i * nb, nb), nb)
        la = la_ref[sl, :]
        suf = sum(dot32(later, t) for t in split3(la))
        tot = [suf[c * CHUNK:c * CHUNK + 1] + la[c * CHUNK:c * CHUNK + 1] for c in range(nc)]
        a = [tot[0]]
        for c in range(1, nc):
            a.append(a[-1] + tot[c])
        qf = q_ref[0, sl, :].astype(F32)
        k_dec = k_ref[0, sl, :].astype(F32) * jnp.exp(suf)
        v = jnp.concatenate([v_ref[0, sl, :], v_ref[1, sl, :]], axis=-1)
        s0 = state_ref[...]

        p = None
        zeros = jnp.zeros((CHUNK, dk), BF16)
        k_dec16 = k_dec.astype(BF16)
        for i_c in range(nc):
            q_i = (qf * per_chunk_rows([jnp.exp(jnp.minimum(a[c] - a[i_c], 0.0))
                                        for c in range(nc)])).astype(BF16)
            k_i = jnp.concatenate([k_dec16[c * CHUNK:(c + 1) * CHUNK] if c == i_c else zeros
                                   for c in range(nc)], axis=0)
            term = lax.dot_general(q_i, k_i, nt, preferred_element_type=F32)
            p = term if p is None else p + term
        p = jnp.where(visible, p, 0.0).astype(BF16)
        q_in = (qf * per_chunk_rows([jnp.exp(a[c]) for c in range(nc)])).astype(BF16)
        ob_ref[sl, :] = dot32(q_in, s0.astype(BF16)) + dot32(p, v)

        k_out = (k_dec * per_chunk_rows([jnp.exp(a[nc - 1] - a[c]) for c in range(nc)])).astype(BF16)
        upd = lax.dot_general(k_out, v, (((0,), (0,)), ((), ())), preferred_element_type=F32)
        decay = jnp.exp(jnp.broadcast_to(a[nc - 1], (LANES, dk)).T)
        state_ref[...] = jnp.concatenate([decay] * (dv // LANES), axis=-1) * s0 + upd
        return carry

    lax.fori_loop(0, n_blocks, body, 0)

    o = ob_ref[...]
    ms = jnp.mean(o * o, axis=-1, keepdims=True)
    on = (o * lax.rsqrt(ms + RMS_EPS)) * ng_ref[...]
    r = jnp.concatenate([r_ref[0], r_ref[1]], axis=-1).astype(F32)
    o_ref[...] = (on * (r * jax.nn.sigmoid(r))).astype(BF16)


def _gla_scan(proj, gz, w_gate_up, b_gate, norm_g, *, batch, seq, lc=1024):
    t = batch * seq
    lc = min(lc, seq)
    assert seq % lc == 0 and lc % GLA_BLOCK == 0 and GLA_BLOCK % CHUNK == 0
    ncb = seq // lc
    dk = SLAB
    dv = 2 * SLAB
    d_out = GLA_HEADS * dv
    rank = w_gate_up.shape[0]
    wg = jnp.zeros((LANES, GLA_HEADS * dk), F32).at[:rank].set(w_gate_up)

    def tok(b, h, c):
        return b * ncb + c

    return pl.pallas_call(
        functools.partial(_gla_kernel, n_blocks=lc // GLA_BLOCK),
        grid=(batch, GLA_HEADS, ncb),
        in_specs=[
            pl.BlockSpec((1, lc, SLAB), lambda b, h, c: (h, tok(b, h, c), 0)),
            pl.BlockSpec((1, lc, SLAB), lambda b, h, c: (GLA_HEADS + h, tok(b, h, c), 0)),
            pl.BlockSpec((2, lc, SLAB), lambda b, h, c: (GLA_HEADS + h, tok(b, h, c), 0)),
            pl.BlockSpec((2, lc, SLAB), lambda b, h, c: (2 * GLA_HEADS + h, tok(b, h, c), 0)),
            pl.BlockSpec((lc, LANES), lambda b, h, c: (tok(b, h, c), 0)),
            pl.BlockSpec((LANES, dk), lambda b, h, c: (0, h)),
            pl.BlockSpec((1, dk), lambda b, h, c: (0, h)),
            pl.BlockSpec((1, dv), lambda b, h, c: (0, 0)),
        ],
        out_specs=pl.BlockSpec((lc, dv), lambda b, h, c: (tok(b, h, c), h)),
        out_shape=jax.ShapeDtypeStruct((t, d_out), BF16),
        scratch_shapes=[pltpu.VMEM((dk, dv), F32), pltpu.VMEM((lc, dk), F32),
                        pltpu.VMEM((lc, dv), F32)],
        compiler_params=_cparams(("parallel", "parallel", "arbitrary")),
    )(proj, proj, proj, proj, gz, wg, b_gate.reshape(1, -1), norm_g.reshape(1, -1))


def _diff_attn_kernel(slope_ref, invct_ref, q_ref, k_ref, v_ref, qfeat_ref, kfeat_ref, lq1_ref, lk1_ref,
                      lq2_ref, lk2_ref, sg_ref, o_ref, vt_ref, dbias_ref, r0_ref, kmax_ref, m_ref, l_ref, acc_ref,
                      *, tq, lambda_init):
    h = pl.program_id(1)
    qi = pl.program_id(2)
    hd = SLAB // 2
    c = slope_ref[h] * LOG2E
    nt = (((1,), (1,)), ((), ()))

    ones16 = jnp.ones((16, hd), BF16)

    def sq_norms(x):
        xf = x.astype(F32)
        sq = (xf * xf * SQ_SLACK).astype(BF16)
        return lax.dot_general(ones16, sq, nt, preferred_element_type=F32)[0:1]

    @pl.when(qi == 0)
    def _():
        def per_block(i, carry):
            sl = pl.ds(pl.multiple_of(i * tq, tq), tq)
            vt_ref[:, sl] = v_ref[0, sl, :].T
            kb = k_ref[0, sl, :]
            return tuple(jnp.maximum(carry[u], sq_norms(kb[:, u * hd:(u + 1) * hd]))
                         for u in range(2))
        zero = jnp.zeros((1, tq), F32)
        kn2 = lax.fori_loop(0, v_ref.shape[1] // tq, per_block, (zero, zero))
        for u in range(2):
            kmax_ref[u] = jnp.broadcast_to(jnp.sqrt(jnp.max(kn2[u], axis=1, keepdims=True)),
                                           (8, LANES))
        ki = lax.broadcasted_iota(I32, (tq, tq), 0)
        qj = lax.broadcasted_iota(I32, (tq, tq), 1)
        bias = c * (qj - jnp.abs(qj - ki)).astype(F32)
        dbias_ref[...] = jnp.where((qj // CHUNK) >= (ki // CHUNK), bias, -jnp.inf)

    q = q_ref[0]
    start = pl.multiple_of(qi * tq, tq)

    k = k_ref[0, pl.ds(start, tq), :]
    vt = vt_ref[:, pl.ds(start, tq)]
    for u in range(2):
        r = lax.dot_general(k[:, u * hd:(u + 1) * hd], q[:, u * hd:(u + 1) * hd], nt,
                            preferred_element_type=F32) + dbias_ref[...]
        m_new = jnp.max(r, axis=0, keepdims=True)
        p = jnp.exp2(r - m_new)
        l_ref[u] = jnp.sum(p, axis=0, keepdims=True)
        acc_ref[u] = jnp.dot(vt, p.astype(BF16), preferred_element_type=F32)
        m_ref[u] = m_new

    qfeat = jnp.broadcast_to(qfeat_ref[0], (tq, LANES))
    kfeat = kfeat_ref[...]
    qa = [jnp.concatenate([q[:, u * hd:(u + 1) * hd], qfeat], axis=-1) for u in range(2)]

    def scores(u, blk):
        st = pl.multiple_of(blk * tq, tq)
        ka = jnp.concatenate([k_ref[0, pl.ds(st, tq), u * hd:(u + 1) * hd], kfeat], axis=-1)
        return lax.dot_general(ka, qa[u], nt, preferred_element_type=F32)

    def accumulate(u, r, vt, off):
        m_prev = m_ref[u]
        m_new = jnp.maximum(m_prev, jnp.max(r, axis=0, keepdims=True) - off)
        alpha = jnp.exp2(m_prev - m_new)
        p = jnp.exp2(r - (m_new + off))
        l_ref[u] = alpha * l_ref[u] + jnp.sum(p, axis=0, keepdims=True)
        acc_ref[u] = alpha * acc_ref[u] + jnp.dot(vt, p.astype(BF16),
                                                  preferred_element_type=F32)
        m_ref[u] = m_new

    r0_ref[...] = scores(0, jnp.maximum(qi - 1, 0))

    def past_block(dd, carry):
        blk = qi - dd
        vt = vt_ref[:, pl.ds(pl.multiple_of(blk * tq, tq), tq)]
        off = c * (dd * tq).astype(F32)
        r1 = scores(1, blk)
        accumulate(0, r0_ref[...], vt, off)
        r0_ref[...] = scores(0, jnp.maximum(blk - 1, 0))
        accumulate(1, r1, vt, off)
        return carry

    gap = None
    for u in range(2):
        bound = (jnp.sqrt(sq_norms(q[:, u * hd:(u + 1) * hd])) * kmax_ref[u][0:1, 0:1]
                 * NORM_SLACK - m_ref[u])
        g = jnp.max(bound, axis=1, keepdims=True)
        gap = g if gap is None else jnp.maximum(gap, g)
    n_need = jnp.ceil((gap + EXP_ZERO) * invct_ref[h])
    n_need = jnp.clip(n_need, 0.0, float(v_ref.shape[1] // tq)).astype(I32)[0, 0]
    n_past = jnp.minimum(qi, n_need)

    def past_pair(i, carry):
        past_block(2 * i + 1, carry)
        return past_block(2 * i + 2, carry)

    lax.fori_loop(0, n_past // 2, past_pair, 0)

    @pl.when(n_past % 2 == 1)
    def _():
        past_block(n_past, 0)

    lam = (jnp.exp(jnp.sum(lq1_ref[...] * lk1_ref[...], axis=-1, keepdims=True))
           - jnp.exp(jnp.sum(lq2_ref[...] * lk2_ref[...], axis=-1, keepdims=True))
           + lambda_init)
    o = acc_ref[0] * (1.0 / l_ref[0]) - lam * (acc_ref[1] * (1.0 / l_ref[1]))
    o = o.T
    ms = jnp.mean(o * o, axis=-1, keepdims=True)
    o = (o * lax.rsqrt(ms + RMS_EPS)) * sg_ref[...]
    o_ref[...] = (o * (1.0 - lambda_init)).astype(BF16)


def _alibi_features(slopes, tq):
    c = slopes * LOG2E
    c_hi = c.astype(BF16).astype(F32)
    c_lo = c - c_hi
    qf = jnp.zeros((slopes.shape[0], 1, LANES), F32)
    qf = qf.at[:, 0, 0].set(LANES * c_hi).at[:, 0, 1].set(c_hi)
    qf = qf.at[:, 0, 2].set(LANES * c_lo).at[:, 0, 3].set(c_lo)
    idx = jnp.arange(tq, dtype=I32)
    hi = (idx // LANES).astype(F32)
    lo = (idx % LANES).astype(F32)
    kf = jnp.zeros((tq, LANES), F32).at[:, 0].set(hi).at[:, 1].set(lo).at[:, 2].set(hi).at[:, 3].set(lo)
    return qf.astype(BF16), kf.astype(BF16)


def _diff_attention(qkv, lq1, lk1, lq2, lk2, subln, *, batch, seq, layer_idx, tq=512):
    t = batch * seq
    tq = min(tq, seq)
    assert seq % tq == 0 and tq % CHUNK == 0
    nq = seq // tq
    lambda_init = 0.8 - 0.6 * math.exp(-0.3 * layer_idx)
    slopes = 2.0 ** (-8.0 * jnp.arange(1, DIFF_HEADS + 1, dtype=F32) / DIFF_HEADS)
    vec = lambda a: a.reshape(1, -1).astype(F32)
    qfeat, kfeat = _alibi_features(slopes, tq)
    grid_spec = pltpu.PrefetchScalarGridSpec(
        num_scalar_prefetch=2,
        grid=(batch, DIFF_HEADS, nq),
        in_specs=[
            pl.BlockSpec((1, tq, SLAB), lambda b, h, i, s, n: (h, b * nq + i, 0)),
            pl.BlockSpec((1, seq, SLAB), lambda b, h, i, s, n: (DIFF_HEADS + h, b, 0)),
            pl.BlockSpec((1, seq, SLAB), lambda b, h, i, s, n: (2 * DIFF_HEADS + h, b, 0)),
            pl.BlockSpec((1, 1, LANES), lambda b, h, i, s, n: (h, 0, 0)),
            pl.BlockSpec((tq, LANES), lambda b, h, i, s, n: (0, 0)),
            pl.BlockSpec((1, SLAB // 2), lambda b, h, i, s, n: (0, 0)),
            pl.BlockSpec((1, SLAB // 2), lambda b, h, i, s, n: (0, 0)),
            pl.BlockSpec((1, SLAB // 2), lambda b, h, i, s, n: (0, 0)),
            pl.BlockSpec((1, SLAB // 2), lambda b, h, i, s, n: (0, 0)),
            pl.BlockSpec((1, SLAB), lambda b, h, i, s, n: (0, 0)),
        ],
        out_specs=pl.BlockSpec((tq, SLAB), lambda b, h, i, s, n: (b * nq + i, h)),
        scratch_shapes=[pltpu.VMEM((SLAB, seq), BF16), pltpu.VMEM((tq, tq), F32),
                        pltpu.VMEM((tq, tq), F32), pltpu.VMEM((2, 8, LANES), F32),
                        pltpu.VMEM((2, 1, tq), F32), pltpu.VMEM((2, 1, tq), F32),
                        pltpu.VMEM((2, SLAB, tq), F32)],
    )
    return pl.pallas_call(
        functools.partial(_diff_attn_kernel, tq=tq, lambda_init=lambda_init),
        grid_spec=grid_spec,
        out_shape=jax.ShapeDtypeStruct((t, DIFF_HEADS * SLAB), BF16),
        compiler_params=_cparams(("parallel", "parallel", "arbitrary")),
    )(slopes, 1.0 / (slopes * (LOG2E * tq)), qkv, qkv, qkv, qfeat, kfeat, vec(lq1), vec(lk1), vec(lq2), vec(lk2), vec(subln))


def _proj_router_kernel(o_ref, w_ref, x_ref, g_ref, wr_ref, x1_ref, ri_ref, rw_ref):
    x1 = x_ref[...] + jnp.dot(o_ref[...], w_ref[...], preferred_element_type=F32)
    x1_ref[...] = x1
    ms = jnp.mean(x1 * x1, axis=-1, keepdims=True)
    hn = ((x1 * lax.rsqrt(ms + RMS_EPS)) * g_ref[...]).astype(BF16)
    logits = jnp.dot(hn, wr_ref[...], preferred_element_type=F32)
    lane = lax.broadcasted_iota(I32, logits.shape, 1)
    neg = -jnp.inf

    def first_max(vals):
        m = jnp.max(vals, axis=-1, keepdims=True)
        idx = jnp.min(jnp.where(vals == m, lane, LANES), axis=-1, keepdims=True)
        return m, idx

    gmask = lane < MOE_GROUPS
    gmax, gidx = first_max(jnp.where(gmask, logits, neg))
    gden = jnp.sum(jnp.where(gmask, jnp.exp(logits - gmax), 0.0), axis=-1, keepdims=True)
    g_w = 1.0 / gden
    lo = MOE_GROUPS + MOE_EXPERTS_PER_GROUP * gidx
    emask = (lane >= lo) & (lane < lo + MOE_EXPERTS_PER_GROUP)
    elog = jnp.where(emask, logits, neg)
    m1, i1 = first_max(elog)
    m2, i2 = first_max(jnp.where(lane == i1, neg, elog))
    tt = jnp.exp(m2 - m1)
    w1 = g_w / (1.0 + tt)
    w2 = w1 * tt
    ri_ref[...] = jnp.where(lane == 0, i1 - MOE_GROUPS, jnp.where(lane == 1, i2 - MOE_GROUPS, 0))
    rw_ref[...] = jnp.where(lane == 0, w1, jnp.where(lane == 1, w2, 0.0))


def _proj_router(o, w_out, x, g_ffn, w_router, *, tm=512):
    t, d = x.shape
    tm = min(tm, t)
    assert t % tm == 0
    return pl.pallas_call(
        _proj_router_kernel,
        grid=(t // tm,),
        in_specs=[
            pl.BlockSpec((tm, d), lambda i: (i, 0)),
            pl.BlockSpec((d, d), lambda i: (0, 0)),
            pl.BlockSpec((tm, d), lambda i: (i, 0)),
            pl.BlockSpec((1, d), lambda i: (0, 0)),
            pl.BlockSpec((d, LANES), lambda i: (0, 0)),
        ],
        out_specs=[
            pl.BlockSpec((tm, d), lambda i: (i, 0)),
            pl.BlockSpec((tm, LANES), lambda i: (i, 0)),
            pl.BlockSpec((tm, LANES), lambda i: (i, 0)),
        ],
        out_shape=[jax.ShapeDtypeStruct((t, d), F32), jax.ShapeDtypeStruct((t, LANES), I32),
                   jax.ShapeDtypeStruct((t, LANES), F32)],
        compiler_params=_cparams(("parallel",)),
    )(o, w_out, x, g_ffn.reshape(1, d), w_router)


def _token_copy(src_hbm, dst_ref, sem, src_tok, dst_tok, rt):
    return pltpu.make_async_copy(src_hbm.at[pl.ds(pl.multiple_of(src_tok * rt, rt), rt), :],
                                 dst_ref.at[pl.ds(dst_tok * rt, rt), :], sem)


def _gather_rows(idx_ref, n, src_hbm, dst_ref, sem, rt=1):
    def body(r, carry):
        _token_copy(src_hbm, dst_ref, sem, idx_ref[0, 0, r], r, rt).start()
        return carry
    lax.fori_loop(0, n, body, 0, unroll=8)


def _wait_rows(n, src_hbm, dst_ref, sem, rt=1):
    pltpu.make_async_copy(src_hbm.at[pl.ds(0, n * rt), :], dst_ref, sem).wait()


def _moe_kernel(eid_ref, tok_ref, tok_next_ref, x_hbm, g_ref, wg32_ref, wu32_ref,
                wd32_ref, o_ref, xbuf, sem, wg_ref, wu_ref, wd_ref, *, blk):
    b = pl.program_id(0)
    nb = pl.num_programs(0)
    slot = b % 2

    @pl.when((b == 0) | (eid_ref[b] != eid_ref[jnp.maximum(b - 1, 0)]))
    def _():
        wg_ref[0] = wg32_ref[0].astype(BF16)
        wu_ref[0] = wu32_ref[0].astype(BF16)
        wd_ref[0] = wd32_ref[0].astype(BF16)

    @pl.when(b == 0)
    def _():
        _gather_rows(tok_ref, blk, x_hbm, xbuf.at[0], sem.at[0])

    _wait_rows(blk, x_hbm, xbuf.at[slot], sem.at[slot])

    x = xbuf[slot]
    ms = jnp.mean(x * x, axis=-1, keepdims=True)
    hn = ((x * lax.rsqrt(ms + RMS_EPS)) * g_ref[...]).astype(BF16)

    def request(quarter):
        for r in range(quarter * blk // 4, (quarter + 1) * blk // 4):
            _token_copy(x_hbm, xbuf.at[1 - slot], sem.at[1 - slot], tok_next_ref[0, 0, r], r,
                        1).start()

    request(0)
    hg = jnp.dot(hn, wg_ref[0], preferred_element_type=F32)
    request(1)
    hu = jnp.dot(hn, wu_ref[0], preferred_element_type=F32)
    request(2)
    act = ((hg * jax.nn.sigmoid(hg)) * hu).astype(BF16)
    out = jnp.dot(act, wd_ref[0], preferred_element_type=F32)
    request(3)
    o_ref[...] = out

    @pl.when(b == nb - 1)
    def _():
        _wait_rows(blk, x_hbm, xbuf.at[1 - slot], sem.at[1 - slot])


def _moe_ffn(x, g_ffn, blk_eid, row_tok, w_gate, w_up, w_down, *, blk, layer):
    t, d = x.shape
    n_rows = row_tok.shape[0]
    n_blocks = n_rows // blk
    dff = w_gate.shape[-1]
    first = layer * MOE_N_EXPERTS
    tok3 = row_tok.reshape(n_blocks, 1, blk)
    grid_spec = pltpu.PrefetchScalarGridSpec(
        num_scalar_prefetch=1,
        grid=(n_blocks,),
        in_specs=[
            pl.BlockSpec((1, 1, blk), lambda b, e: (b, 0, 0), memory_space=pltpu.SMEM),
            pl.BlockSpec((1, 1, blk), lambda b, e: (jnp.minimum(b + 1, n_blocks - 1), 0, 0),
                         memory_space=pltpu.SMEM),
            pl.BlockSpec(memory_space=pl.ANY),
            pl.BlockSpec((1, d), lambda b, e: (0, 0)),
            pl.BlockSpec((1, d, dff), lambda b, e: (first + e[b], 0, 0)),
            pl.BlockSpec((1, d, dff), lambda b, e: (first + e[b], 0, 0)),
            pl.BlockSpec((1, dff, d), lambda b, e: (first + e[b], 0, 0)),
        ],
        out_specs=pl.BlockSpec((blk, d), lambda b, e: (b, 0)),
        scratch_shapes=[pltpu.VMEM((2, blk, d), F32), pltpu.SemaphoreType.DMA((2,)),
                        pltpu.VMEM((1, d, dff), BF16), pltpu.VMEM((1, d, dff), BF16),
                        pltpu.VMEM((1, dff, d), BF16)],
    )
    return pl.pallas_call(
        functools.partial(_moe_kernel, blk=blk),
        grid_spec=grid_spec,
        out_shape=jax.ShapeDtypeStruct((n_rows, d), F32),
        compiler_params=_cparams(("arbitrary",)),
    )(blk_eid, tok3, tok3, x, g_ffn.reshape(1, d), w_gate, w_up, w_down)


def _combine_kernel(pos_ref, pos_next_ref, x_ref, rw_ref, y_hbm, g_ref, o_ref, ybuf, sem,
                    *, tc, final_norm):
    i = pl.program_id(0)
    n = pl.num_programs(0)
    slot = i % 2

    @pl.when(i == 0)
    def _():
        _gather_rows(pos_ref, 2 * tc, y_hbm, ybuf.at[0], sem.at[0])

    _wait_rows(2 * tc, y_hbm, ybuf.at[slot], sem.at[slot])

    for r in range(2 * tc):
        _token_copy(y_hbm, ybuf.at[1 - slot], sem.at[1 - slot], pos_next_ref[0, 0, r], r, 1).start()

    rw = rw_ref[...]
    y = x_ref[...] + rw[:, 0:1] * ybuf[slot, pl.ds(0, tc), :] + rw[:, 1:2] * ybuf[slot, pl.ds(tc, tc), :]
    if final_norm:
        ms = jnp.mean(y * y, axis=-1, keepdims=True)
        y = (y * lax.rsqrt(ms + RMS_EPS)) * g_ref[...]
    o_ref[...] = y

    @pl.when(i == n - 1)
    def _():
        _wait_rows(2 * tc, y_hbm, ybuf.at[1 - slot], sem.at[1 - slot])


def _combine(x, route_w, pos, y_sorted, g_final, *, final_norm, tc=256):
    t, d = x.shape
    tc = min(tc, t)
    assert t % tc == 0
    nt = t // tc
    pos3 = pos.reshape(nt, tc, 2).transpose(0, 2, 1).reshape(nt, 1, 2 * tc)
    return pl.pallas_call(
        functools.partial(_combine_kernel, tc=tc, final_norm=final_norm),
        grid=(nt,),
        in_specs=[
            pl.BlockSpec((1, 1, 2 * tc), lambda i: (i, 0, 0), memory_space=pltpu.SMEM),
            pl.BlockSpec((1, 1, 2 * tc), lambda i: (jnp.minimum(i + 1, nt - 1), 0, 0),
                         memory_space=pltpu.SMEM),
            pl.BlockSpec((tc, d), lambda i: (i, 0)),
            pl.BlockSpec((tc, LANES), lambda i: (i, 0)),
            pl.BlockSpec(memory_space=pl.ANY),
            pl.BlockSpec((1, d), lambda i: (0, 0)),
        ],
        out_specs=pl.BlockSpec((tc, d), lambda i: (i, 0)),
        out_shape=jax.ShapeDtypeStruct((t, d), F32),
        scratch_shapes=[pltpu.VMEM((2, 2 * tc, d), F32), pltpu.SemaphoreType.DMA((2,))],
        compiler_params=_cparams(("arbitrary",)),
    )(pos3, pos3, x, route_w, y_sorted, g_final.reshape(1, d))


def _dispatch(route_i, blk):
    t = route_i.shape[0]
    eid = route_i[:, :2].reshape(-1)
    n_assign = eid.shape[0]
    onehot = (eid[:, None] == jnp.arange(MOE_N_EXPERTS, dtype=I32)[None, :]).astype(F32)
    tile = 256
    oh = onehot.reshape(n_assign // tile, tile, MOE_N_EXPERTS)
    within = jnp.einsum('ij,tje->tie', jnp.tril(jnp.ones((tile, tile), F32)), oh)
    tile_tot = within[:, -1, :]
    tile_off = jnp.cumsum(tile_tot, axis=0) - tile_tot
    csum = within + tile_off[:, None, :]
    rank = (jnp.sum(csum * oh, axis=2) - 1.0).reshape(n_assign).astype(I32)
    counts = (tile_off[-1] + tile_tot[-1]).astype(I32)
    padded = ((counts + blk - 1) // blk) * blk
    pend = jnp.cumsum(padded)
    pstart = pend - padded
    dest = (pstart[eid] + rank).astype(I32)
    n_rows = n_assign + MOE_N_EXPERTS * blk
    n_blocks = n_rows // blk
    row_tok = jnp.zeros((n_rows,), I32).at[dest].set(jnp.arange(n_assign, dtype=I32) // 2)
    blk_start = jnp.arange(n_blocks, dtype=I32) * blk
    blk_eid = jnp.minimum(jnp.sum((pend[None, :] <= blk_start[:, None]).astype(I32), axis=1),
                          MOE_N_EXPERTS - 1)
    return row_tok, blk_eid, dest.reshape(t, 2)


def _router_weights(w_group, w_expert):
    d = w_group.shape[0]
    w = jnp.zeros((d, LANES), F32)
    w = w.at[:, :MOE_GROUPS].set(w_group).at[:, MOE_GROUPS:MOE_GROUPS + MOE_N_EXPERTS].set(w_expert)
    return w.astype(BF16)


def _moe_layer(x1, route_i, route_w, g_ffn, w_gate, w_up, w_down, g_final, *, layer, final_norm,
               blk=256):
    row_tok, blk_eid, pos = _dispatch(route_i, blk)
    stack = lambda w: w.reshape((-1,) + w.shape[2:])
    y_sorted = _moe_ffn(x1, g_ffn, blk_eid, row_tok, stack(w_gate), stack(w_up),
                        stack(w_down), blk=blk, layer=layer)
    return _combine(x1, route_w, pos, y_sorted, g_final, final_norm=final_norm)


def kernel(x, norm_mix, norm_ffn, gla_w_in, gla_w_gate_up, gla_b_gate, gla_norm, gla_w_out,
           diff_w_in, diff_lambda_q1, diff_lambda_k1, diff_lambda_q2, diff_lambda_k2,
           diff_subln, diff_w_out, moe_w_group, moe_w_expert, moe_w_gate, moe_w_up,
           moe_w_down, final_norm):
    batch, seq, d = x.shape
    t = batch * seq
    xf = x.reshape(t, d)

    dk_total = d // 2
    n_main = 2 * dk_total + 2 * d
    w_in = gla_w_in[0]
    gla_dk = dk_total // GLA_HEADS
    colscale = jnp.ones((n_main,), F32).at[:dk_total].set(gla_dk ** -0.5)
    rank = w_in.shape[1] - n_main
    w_gz = jnp.zeros((d, LANES), F32).at[:, :rank].set(w_in[:, n_main:]).astype(BF16)
    proj, gz = _norm_matmul(xf, norm_mix[0], w_in[:, :n_main].astype(BF16), colscale, w_gz)
    o = _gla_scan(proj, gz, gla_w_gate_up[0], gla_b_gate[0], gla_norm[0], batch=batch, seq=seq)
    x1, ri, rw = _proj_router(o, gla_w_out[0].astype(BF16), xf, norm_ffn[0],
                              _router_weights(moe_w_group[0], moe_w_expert[0]))
    x2 = _moe_layer(x1, ri, rw, norm_ffn[0], moe_w_gate, moe_w_up, moe_w_down,
                    final_norm, layer=0, final_norm=False)

    hd = d // DIFF_HEADS // 2
    colscale = jnp.ones((3 * d,), F32).at[:d].set(hd ** -0.5 * LOG2E)
    qkv = _norm_matmul(x2, norm_mix[1], diff_w_in[0].astype(BF16), colscale)
    o = _diff_attention(qkv, diff_lambda_q1[0], diff_lambda_k1[0], diff_lambda_q2[0],
                        diff_lambda_k2[0], diff_subln[0], batch=batch, seq=seq, layer_idx=1)
    x3, ri, rw = _proj_router(o, diff_w_out[0].astype(BF16), x2, norm_ffn[1],
                              _router_weights(moe_w_group[1], moe_w_expert[1]))
    out = _moe_layer(x3, ri, rw, norm_ffn[1], moe_w_gate, moe_w_up, moe_w_down,
                     final_norm, layer=1, final_norm=True)
    return out.reshape(batch, seq, d)
```

```python
import functools
import math

import jax
import jax.numpy as jnp
from jax import lax
from jax.experimental import pallas as pl
from jax.experimental.pallas import tpu as pltpu

F32 = jnp.float32
BF16 = jnp.bfloat16
I32 = jnp.int32

RMS_EPS = 1e-6
CHUNK = 64
GLA_BLOCK = 256
GLA_HEADS = 4
GLA_GATE_TAU = 16.0
DIFF_HEADS = 8
MOE_GROUPS = 4
MOE_EXPERTS_PER_GROUP = 8
MOE_N_EXPERTS = MOE_GROUPS * MOE_EXPERTS_PER_GROUP

LANES = 128
SLAB = 256
VMEM_LIMIT = 56 * 1024 * 1024
LOG2E = 1.4426950408889634
NEG_BIG = -1e30
EXP_ZERO = 160.0
NORM_SLACK = 1.0 + 2.0 ** -10
SQ_SLACK = 1.0 + 2.0 ** -7

HIGHEST = lax.Precision.HIGHEST


def _cparams(sem):
    return pltpu.CompilerParams(dimension_semantics=sem, vmem_limit_bytes=VMEM_LIMIT)


def _norm_matmul_kernel(x_ref, g_ref, w_ref, cs_ref, *rest, n_slab, has_aux):
    if has_aux:
        w2_ref, o_ref, o2_ref, hn_ref = rest
    else:
        o_ref, hn_ref = rest

    @pl.when(pl.program_id(1) == 0)
    def _():
        x = x_ref[...]
        ms = jnp.mean(x * x, axis=-1, keepdims=True)
        hn = ((x * lax.rsqrt(ms + RMS_EPS)) * g_ref[...]).astype(BF16)
        hn_ref[...] = hn
        if has_aux:
            o2_ref[...] = jnp.dot(hn, w2_ref[...], preferred_element_type=F32)

    acc = jnp.dot(hn_ref[...], w_ref[...], preferred_element_type=F32) * cs_ref[...]
    for s in range(n_slab):
        o_ref[s] = acc[:, s * SLAB:(s + 1) * SLAB].astype(BF16)


def _norm_matmul(x, g, w, colscale, w_aux=None, *, tm=1024, tn=1024):
    t, d = x.shape
    n = w.shape[1]
    tm = min(tm, t)
    assert t % tm == 0 and n % tn == 0 and tn % SLAB == 0
    n_slab = tn // SLAB
    has_aux = w_aux is not None
    in_specs = [
        pl.BlockSpec((tm, d), lambda i, j: (i, 0)),
        pl.BlockSpec((1, d), lambda i, j: (0, 0)),
        pl.BlockSpec((d, tn), lambda i, j: (0, j)),
        pl.BlockSpec((1, tn), lambda i, j: (0, j)),
    ]
    out_shape = [jax.ShapeDtypeStruct((n // SLAB, t, SLAB), BF16)]
    out_specs = [pl.BlockSpec((n_slab, tm, SLAB), lambda i, j: (j, i, 0))]
    args = [x, g.reshape(1, d), w, colscale.reshape(1, n)]
    if has_aux:
        in_specs.append(pl.BlockSpec((d, LANES), lambda i, j: (0, 0)))
        out_shape.append(jax.ShapeDtypeStruct((t, LANES), F32))
        out_specs.append(pl.BlockSpec((tm, LANES), lambda i, j: (i, 0)))
        args.append(w_aux)
    res = pl.pallas_call(
        functools.partial(_norm_matmul_kernel, n_slab=n_slab, has_aux=has_aux),
        grid=(t // tm, n // tn),
        in_specs=in_specs,
        out_specs=out_specs,
        out_shape=out_shape,
        scratch_shapes=[pltpu.VMEM((tm, d), BF16)],
        compiler_params=_cparams(("parallel", "arbitrary")),
    )(*args)
    return res if has_aux else res[0]


def _gla_kernel(q_ref, k_ref, v_ref, r_ref, gz_ref, wg_ref, bg_ref, ng_ref, o_ref,
                state_ref, la_ref, ob_ref, *, n_blocks):
    @pl.when(pl.program_id(2) == 0)
    def _():
        state_ref[...] = jnp.zeros_like(state_ref)

    def split3(x):
        hi = x.astype(BF16)
        r1 = x - hi.astype(F32)
        mid = r1.astype(BF16)
        return hi, mid, (r1 - mid.astype(F32)).astype(BF16)

    def dot32(a, b):
        return jnp.dot(a, b, preferred_element_type=F32)

    gh, gm, _ = split3(gz_ref[...])
    wh, wm, _ = split3(wg_ref[...])
    z = dot32(gh, wh) + dot32(gm, wh) + dot32(gh, wm) + bg_ref[...]
    la_ref[...] = -(jnp.maximum(-z, 0.0) + jnp.log1p(jnp.exp(-jnp.abs(z)))) * (1.0 / GLA_GATE_TAU)

    nb = GLA_BLOCK
    nc = nb // CHUNK
    row = lax.broadcasted_iota(I32, (nb, nb), 0)
    col = lax.broadcasted_iota(I32, (nb, nb), 1)
    same_chunk = (row // CHUNK) == (col // CHUNK)
    later = (same_chunk & (col > row)).astype(BF16)
    visible = (row // CHUNK) >= (col // CHUNK)
    dk = state_ref.shape[0]
    dv = state_ref.shape[1]
    nt = (((1,), (1,)), ((), ()))

    def per_chunk_rows(vals):
        return jnp.concatenate([jnp.broadcast_to(x, (CHUNK, dk)) for x in vals], axis=0)

    def body(i, carry):
        sl = pl.ds(pl.multiple_of(i * nb, nb), nb)
        la = la_ref[sl, :]
        suf = sum(dot32(later, t) for t in split3(la))
        tot = [suf[c * CHUNK:c * CHUNK + 1] + la[c * CHUNK:c * CHUNK + 1] for c in range(nc)]
        a = [tot[0]]
        for c in range(1, nc):
            a.append(a[-1] + tot[c])
        qf = q_ref[0, sl, :].astype(F32)
        k_dec = k_ref[0, sl, :].astype(F32) * jnp.exp(suf)
        v = jnp.concatenate([v_ref[0, sl, :], v_ref[1, sl, :]], axis=-1)
        s0 = state_ref[...]

        p = None
        zeros = jnp.zeros((CHUNK, dk), BF16)
        k_dec16 = k_dec.astype(BF16)
        for i_c in range(nc):
            q_i = (qf * per_chunk_rows([jnp.exp(jnp.minimum(a[c] - a[i_c], 0.0))
                                        for c in range(nc)])).astype(BF16)
            k_i = jnp.concatenate([k_dec16[c * CHUNK:(c + 1) * CHUNK] if c == i_c else zeros
                                   for c in range(nc)], axis=0)
            term = lax.dot_general(q_i, k_i, nt, preferred_element_type=F32)
            p = term if p is None else p + term
        p = jnp.where(visible, p, 0.0).astype(BF16)
        q_in = (qf * per_chunk_rows([jnp.exp(a[c]) for c in range(nc)])).astype(BF16)
        ob_ref[sl, :] = dot32(q_in, s0.astype(BF16)) + dot32(p, v)

        k_out = (k_dec * per_chunk_rows([jnp.exp(a[nc - 1] - a[c]) for c in range(nc)])).astype(BF16)
        upd = lax.dot_general(k_out, v, (((0,), (0,)), ((), ())), preferred_element_type=F32)
        decay = jnp.exp(jnp.broadcast_to(a[nc - 1], (LANES, dk)).T)
        state_ref[...] = jnp.concatenate([decay] * (dv // LANES), axis=-1) * s0 + upd
        return carry

    lax.fori_loop(0, n_blocks, body, 0)

    o = ob_ref[...]
    ms = jnp.mean(o * o, axis=-1, keepdims=True)
    on = (o * lax.rsqrt(ms + RMS_EPS)) * ng_ref[...]
    r = jnp.concatenate([r_ref[0], r_ref[1]], axis=-1).astype(F32)
    o_ref[...] = (on * (r * jax.nn.sigmoid(r))).astype(BF16)


def _gla_scan(proj, gz, w_gate_up, b_gate, norm_g, *, batch, seq, lc=1024):
    t = batch * seq
    lc = min(lc, seq)
    assert seq % lc == 0 and lc % GLA_BLOCK == 0 and GLA_BLOCK % CHUNK == 0
    ncb = seq // lc
    dk = SLAB
    dv = 2 * SLAB
    d_out = GLA_HEADS * dv
    rank = w_gate_up.shape[0]
    wg = jnp.zeros((LANES, GLA_HEADS * dk), F32).at[:rank].set(w_gate_up)

    def tok(b, h, c):
        return b * ncb + c

    return pl.pallas_call(
        functools.partial(_gla_kernel, n_blocks=lc // GLA_BLOCK),
        grid=(batch, GLA_HEADS, ncb),
        in_specs=[
            pl.BlockSpec((1, lc, SLAB), lambda b, h, c: (h, tok(b, h, c), 0)),
            pl.BlockSpec((1, lc, SLAB), lambda b, h, c: (GLA_HEADS + h, tok(b, h, c), 0)),
            pl.BlockSpec((2, lc, SLAB), lambda b, h, c: (GLA_HEADS + h, tok(b, h, c), 0)),
            pl.BlockSpec((2, lc, SLAB), lambda b, h, c: (2 * GLA_HEADS + h, tok(b, h, c), 0)),
            pl.BlockSpec((lc, LANES), lambda b, h, c: (tok(b, h, c), 0)),
            pl.BlockSpec((LANES, dk), lambda b, h, c: (0, h)),
            pl.BlockSpec((1, dk), lambda b, h, c: (0, h)),
            pl.BlockSpec((1, dv), lambda b, h, c: (0, 0)),
        ],
        out_specs=pl.BlockSpec((lc, dv), lambda b, h, c: (tok(b, h, c), h)),
        out_shape=jax.ShapeDtypeStruct((t, d_out), BF16),
        scratch_shapes=[pltpu.VMEM((dk, dv), F32), pltpu.VMEM((lc, dk), F32),
                        pltpu.VMEM((lc, dv), F32)],
        compiler_params=_cparams(("parallel", "parallel", "arbitrary")),
    )(proj, proj, proj, proj, gz, wg, b_gate.reshape(1, -1), norm_g.reshape(1, -1))


def _diff_attn_kernel(slope_ref, invct_ref, q_ref, k_ref, v_ref, qfeat_ref, kfeat_ref, lq1_ref, lk1_ref,
                      lq2_ref, lk2_ref, sg_ref, o_ref, vt_ref, dbias_ref, r0_ref, kmax_ref, m_ref, l_ref, acc_ref,
                      *, tq, lambda_init):
    h = pl.program_id(1)
    qi = pl.program_id(2)
    hd = SLAB // 2
    c = slope_ref[h] * LOG2E
    nt = (((1,), (1,)), ((), ()))

    ones16 = jnp.ones((16, hd), BF16)

    def sq_norms(x):
        xf = x.astype(F32)
        sq = (xf * xf * SQ_SLACK).astype(BF16)
        return lax.dot_general(ones16, sq, nt, preferred_element_type=F32)[0:1]

    @pl.when(qi == 0)
    def _():
        def per_block(i, carry):
            sl = pl.ds(pl.multiple_of(i * tq, tq), tq)
            vt_ref[:, sl] = v_ref[0, sl, :].T
            kb = k_ref[0, sl, :]
            return tuple(jnp.maximum(carry[u], sq_norms(kb[:, u * hd:(u + 1) * hd]))
                         for u in range(2))
        zero = jnp.zeros((1, tq), F32)
        kn2 = lax.fori_loop(0, v_ref.shape[1] // tq, per_block, (zero, zero))
        for u in range(2):
            kmax_ref[u] = jnp.broadcast_to(jnp.sqrt(jnp.max(kn2[u], axis=1, keepdims=True)),
                                           (8, LANES))
        ki = lax.broadcasted_iota(I32, (tq, tq), 0)
        qj = lax.broadcasted_iota(I32, (tq, tq), 1)
        bias = c * (qj - jnp.abs(qj - ki)).astype(F32)
        dbias_ref[...] = jnp.where((qj // CHUNK) >= (ki // CHUNK), bias, -jnp.inf)

    q = q_ref[0]
    start = pl.multiple_of(qi * tq, tq)

    k = k_ref[0, pl.ds(start, tq), :]
    vt = vt_ref[:, pl.ds(start, tq)]
    for u in range(2):
        r = lax.dot_general(k[:, u * hd:(u + 1) * hd], q[:, u * hd:(u + 1) * hd], nt,
                            preferred_element_type=F32) + dbias_ref[...]
        m_new = jnp.max(r, axis=0, keepdims=True)
        p = jnp.exp2(r - m_new)
        l_ref[u] = jnp.sum(p, axis=0, keepdims=True)
        acc_ref[u] = jnp.dot(vt, p.astype(BF16), preferred_element_type=F32)
        m_ref[u] = m_new

    qfeat = jnp.broadcast_to(qfeat_ref[0], (tq, LANES))
    kfeat = kfeat_ref[...]
    qa = [jnp.concatenate([q[:, u * hd:(u + 1) * hd], qfeat], axis=-1) for u in range(2)]

    def scores(u, blk):
        st = pl.multiple_of(blk * tq, tq)
        ka = jnp.concatenate([k_ref[0, pl.ds(st, tq), u * hd:(u + 1) * hd], kfeat], axis=-1)
        return lax.dot_general(ka, qa[u], nt, preferred_element_type=F32)

    def accumulate(u, r, vt, off):
        m_prev = m_ref[u]
        m_new = jnp.maximum(m_prev, jnp.max(r, axis=0, keepdims=True) - off)
        alpha = jnp.exp2(m_prev - m_new)
        p = jnp.exp2(r - (m_new + off))
        l_ref[u] = alpha * l_ref[u] + jnp.sum(p, axis=0, keepdims=True)
        acc_ref[u] = alpha * acc_ref[u] + jnp.dot(vt, p.astype(BF16),
                                                  preferred_element_type=F32)
        m_ref[u] = m_new

    r0_ref[...] = scores(0, jnp.maximum(qi - 1, 0))

    def past_block(dd, carry):
        blk = qi - dd
        vt = vt_ref[:, pl.ds(pl.multiple_of(blk * tq, tq), tq)]
        off = c * (dd * tq).astype(F32)
        r1 = scores(1, blk)
        accumulate(0, r0_ref[...], vt, off)
        r0_ref[...] = scores(0, jnp.maximum(blk - 1, 0))
        accumulate(1, r1, vt, off)
        return carry

    gap = None
    for u in range(2):
        bound = (jnp.sqrt(sq_norms(q[:, u * hd:(u + 1) * hd])) * kmax_ref[u][0:1, 0:1]
                 * NORM_SLACK - m_ref[u])
        g = jnp.max(bound, axis=1, keepdims=True)
        gap = g if gap is None else jnp.maximum(gap, g)
    n_need = jnp.ceil((gap + EXP_ZERO) * invct_ref[h])
    n_need = jnp.clip(n_need, 0.0, float(v_ref.shape[1] // tq)).astype(I32)[0, 0]
    n_past = jnp.minimum(qi, n_need)

    def past_pair(i, carry):
        past_block(2 * i + 1, carry)
        return past_block(2 * i + 2, carry)

    lax.fori_loop(0, n_past // 2, past_pair, 0)

    @pl.when(n_past % 2 == 1)
    def _():
        past_block(n_past, 0)

    lam = (jnp.exp(jnp.sum(lq1_ref[...] * lk1_ref[...], axis=-1, keepdims=True))
           - jnp.exp(jnp.sum(lq2_ref[...] * lk2_ref[...], axis=-1, keepdims=True))
           + lambda_init)
    o = acc_ref[0] * (1.0 / l_ref[0]) - lam * (acc_ref[1] * (1.0 / l_ref[1]))
    o = o.T
    ms = jnp.mean(o * o, axis=-1, keepdims=True)
    o = (o * lax.rsqrt(ms + RMS_EPS)) * sg_ref[...]
    o_ref[...] = (o * (1.0 - lambda_init)).astype(BF16)


def _alibi_features(slopes, tq):
    c = slopes * LOG2E
    c_hi = c.astype(BF16).astype(F32)
    c_lo = c - c_hi
    qf = jnp.zeros((slopes.shape[0], 1, LANES), F32)
    qf = qf.at[:, 0, 0].set(LANES * c_hi).at[:, 0, 1].set(c_hi)
    qf = qf.at[:, 0, 2].set(LANES * c_lo).at[:, 0, 3].set(c_lo)
    idx = jnp.arange(tq, dtype=I32)
    hi = (idx // LANES).astype(F32)
    lo = (idx % LANES).astype(F32)
    kf = jnp.zeros((tq, LANES), F32).at[:, 0].set(hi).at[:, 1].set(lo).at[:, 2].set(hi).at[:, 3].set(lo)
    return qf.astype(BF16), kf.astype(BF16)


def _diff_attention(qkv, lq1, lk1, lq2, lk2, subln, *, batch, seq, layer_idx, tq=512):
    t = batch * seq
    tq = min(tq, seq)
    assert seq % tq == 0 and tq % CHUNK == 0
    nq = seq // tq
    lambda_init = 0.8 - 0.6 * math.exp(-0.3 * layer_idx)
    slopes = 2.0 ** (-8.0 * jnp.arange(1, DIFF_HEADS + 1, dtype=F32) / DIFF_HEADS)
    vec = lambda a: a.reshape(1, -1).astype(F32)
    qfeat, kfeat = _alibi_features(slopes, tq)
    grid_spec = pltpu.PrefetchScalarGridSpec(
        num_scalar_prefetch=2,
        grid=(batch, DIFF_HEADS, nq),
        in_specs=[
            pl.BlockSpec((1, tq, SLAB), lambda b, h, i, s, n: (h, b * nq + i, 0)),
            pl.BlockSpec((1, seq, SLAB), lambda b, h, i, s, n: (DIFF_HEADS + h, b, 0)),
            pl.BlockSpec((1, seq, SLAB), lambda b, h, i, s, n: (2 * DIFF_HEADS + h, b, 0)),
            pl.BlockSpec((1, 1, LANES), lambda b, h, i, s, n: (h, 0, 0)),
            pl.BlockSpec((tq, LANES), lambda b, h, i, s, n: (0, 0)),
            pl.BlockSpec((1, SLAB // 2), lambda b, h, i, s, n: (0, 0)),
            pl.BlockSpec((1, SLAB // 2), lambda b, h, i, s, n: (0, 0)),
            pl.BlockSpec((1, SLAB // 2), lambda b, h, i, s, n: (0, 0)),
            pl.BlockSpec((1, SLAB // 2), lambda b, h, i, s, n: (0, 0)),
            pl.BlockSpec((1, SLAB), lambda b, h, i, s, n: (0, 0)),
        ],
        out_specs=pl.BlockSpec((tq, SLAB), lambda b, h, i, s, n: (b * nq + i, h)),
        scratch_shapes=[pltpu.VMEM((SLAB, seq), BF16), pltpu.VMEM((tq, tq), F32),
                        pltpu.VMEM((tq, tq), F32), pltpu.VMEM((2, 8, LANES), F32),
                        pltpu.VMEM((2, 1, tq), F32), pltpu.VMEM((2, 1, tq), F32),
                        pltpu.VMEM((2, SLAB, tq), F32)],
    )
    return pl.pallas_call(
        functools.partial(_diff_attn_kernel, tq=tq, lambda_init=lambda_init),
        grid_spec=grid_spec,
        out_shape=jax.ShapeDtypeStruct((t, DIFF_HEADS * SLAB), BF16),
        compiler_params=_cparams(("parallel", "parallel", "arbitrary")),
    )(slopes, 1.0 / (slopes * (LOG2E * tq)), qkv, qkv, qkv, qfeat, kfeat, vec(lq1), vec(lk1), vec(lq2), vec(lk2), vec(subln))


def _proj_router_kernel(o_ref, w_ref, x_ref, g_ref, wr_ref, x1_ref, ri_ref, rw_ref):
    x1 = x_ref[...] + jnp.dot(o_ref[...], w_ref[...], preferred_element_type=F32)
    x1_ref[...] = x1
    ms = jnp.mean(x1 * x1, axis=-1, keepdims=True)
    hn = ((x1 * lax.rsqrt(ms + RMS_EPS)) * g_ref[...]).astype(BF16)
    logits = jnp.dot(hn, wr_ref[...], preferred_element_type=F32)
    lane = lax.broadcasted_iota(I32, logits.shape, 1)
    neg = -jnp.inf

    def first_max(vals):
        m = jnp.max(vals, axis=-1, keepdims=True)
        idx = jnp.min(jnp.where(vals == m, lane, LANES), axis=-1, keepdims=True)
        return m, idx

    gmask = lane < MOE_GROUPS
    gmax, gidx = first_max(jnp.where(gmask, logits, neg))
    gden = jnp.sum(jnp.where(gmask, jnp.exp(logits - gmax), 0.0), axis=-1, keepdims=True)
    g_w = 1.0 / gden
    lo = MOE_GROUPS + MOE_EXPERTS_PER_GROUP * gidx
    emask = (lane >= lo) & (lane < lo + MOE_EXPERTS_PER_GROUP)
    elog = jnp.where(emask, logits, neg)
    m1, i1 = first_max(elog)
    m2, i2 = first_max(jnp.where(lane == i1, neg, elog))
    tt = jnp.exp(m2 - m1)
    w1 = g_w / (1.0 + tt)
    w2 = w1 * tt
    ri_ref[...] = jnp.where(lane == 0, i1 - MOE_GROUPS, jnp.where(lane == 1, i2 - MOE_GROUPS, 0))
    rw_ref[...] = jnp.where(lane == 0, w1, jnp.where(lane == 1, w2, 0.0))


def _proj_router(o, w_out, x, g_ffn, w_router, *, tm=512):
    t, d = x.shape
    tm = min(tm, t)
    assert t % tm == 0
    return pl.pallas_call(
        _proj_router_kernel,
        grid=(t // tm,),
        in_specs=[
            pl.BlockSpec((tm, d), lambda i: (i, 0)),
            pl.BlockSpec((d, d), lambda i: (0, 0)),
            pl.BlockSpec((tm, d), lambda i: (i, 0)),
            pl.BlockSpec((1, d), lambda i: (0, 0)),
            pl.BlockSpec((d, LANES), lambda i: (0, 0)),
        ],
        out_specs=[
            pl.BlockSpec((tm, d), lambda i: (i, 0)),
            pl.BlockSpec((tm, LANES), lambda i: (i, 0)),
            pl.BlockSpec((tm, LANES), lambda i: (i, 0)),
        ],
        out_shape=[jax.ShapeDtypeStruct((t, d), F32), jax.ShapeDtypeStruct((t, LANES), I32),
                   jax.ShapeDtypeStruct((t, LANES), F32)],
        compiler_params=_cparams(("parallel",)),
    )(o, w_out, x, g_ffn.reshape(1, d), w_router)


def _token_copy(src_hbm, dst_ref, sem, src_tok, dst_tok, rt):
    return pltpu.make_async_copy(src_hbm.at[pl.ds(pl.multiple_of(src_tok * rt, rt), rt), :],
                                 dst_ref.at[pl.ds(dst_tok * rt, rt), :], sem)


def _gather_rows(idx_ref, n, src_hbm, dst_ref, sem, rt=1):
    def body(r, carry):
        _token_copy(src_hbm, dst_ref, sem, idx_ref[0, 0, r], r, rt).start()
        return carry
    lax.fori_loop(0, n, body, 0, unroll=8)


def _wait_rows(n, src_hbm, dst_ref, sem, rt=1):
    pltpu.make_async_copy(src_hbm.at[pl.ds(0, n * rt), :], dst_ref, sem).wait()


def _moe_kernel(eid_ref, tok_ref, tok_next_ref, x_hbm, g_ref, wg32_ref, wu32_ref,
                wd32_ref, o_ref, xbuf, sem, wg_ref, wu_ref, wd_ref, *, blk):
    b = pl.program_id(0)
    nb = pl.num_programs(0)
    slot = b % 2

    @pl.when((b == 0) | (eid_ref[b] != eid_ref[jnp.maximum(b - 1, 0)]))
    def _():
        wg_ref[0] = wg32_ref[0].astype(BF16)
        wu_ref[0] = wu32_ref[0].astype(BF16)
        wd_ref[0] = wd32_ref[0].astype(BF16)

    @pl.when(b == 0)
    def _():
        _gather_rows(tok_ref, blk, x_hbm, xbuf.at[0], sem.at[0])

    _wait_rows(blk, x_hbm, xbuf.at[slot], sem.at[slot])

    x = xbuf[slot]
    ms = jnp.mean(x * x, axis=-1, keepdims=True)
    hn = ((x * lax.rsqrt(ms + RMS_EPS)) * g_ref[...]).astype(BF16)

    def request(quarter):
        for r in range(quarter * blk // 4, (quarter + 1) * blk // 4):
            _token_copy(x_hbm, xbuf.at[1 - slot], sem.at[1 - slot], tok_next_ref[0, 0, r], r,
                        1).start(priority=r % 2)

    request(0)
    hg = jnp.dot(hn, wg_ref[0], preferred_element_type=F32)
    request(1)
    hu = jnp.dot(hn, wu_ref[0], preferred_element_type=F32)
    request(2)
    act = ((hg * jax.nn.sigmoid(hg)) * hu).astype(BF16)
    out = jnp.dot(act, wd_ref[0], preferred_element_type=F32)
    request(3)
    o_ref[...] = out

    @pl.when(b == nb - 1)
    def _():
        _wait_rows(blk, x_hbm, xbuf.at[1 - slot], sem.at[1 - slot])


def _moe_ffn(x, g_ffn, blk_eid, row_tok, w_gate, w_up, w_down, *, blk, layer):
    t, d = x.shape
    n_rows = row_tok.shape[0]
    n_blocks = n_rows // blk
    dff = w_gate.shape[-1]
    first = layer * MOE_N_EXPERTS
    tok3 = row_tok.reshape(n_blocks, 1, blk)
    grid_spec = pltpu.PrefetchScalarGridSpec(
        num_scalar_prefetch=1,
        grid=(n_blocks,),
        in_specs=[
            pl.BlockSpec((1, 1, blk), lambda b, e: (b, 0, 0), memory_space=pltpu.SMEM),
            pl.BlockSpec((1, 1, blk), lambda b, e: (jnp.minimum(b + 1, n_blocks - 1), 0, 0),
                         memory_space=pltpu.SMEM),
            pl.BlockSpec(memory_space=pl.ANY),
            pl.BlockSpec((1, d), lambda b, e: (0, 0)),
            pl.BlockSpec((1, d, dff), lambda b, e: (first + e[b], 0, 0)),
            pl.BlockSpec((1, d, dff), lambda b, e: (first + e[b], 0, 0)),
            pl.BlockSpec((1, dff, d), lambda b, e: (first + e[b], 0, 0)),
        ],
        out_specs=pl.BlockSpec((blk, d), lambda b, e: (b, 0)),
        scratch_shapes=[pltpu.VMEM((2, blk, d), F32), pltpu.SemaphoreType.DMA((2,)),
                        pltpu.VMEM((1, d, dff), BF16), pltpu.VMEM((1, d, dff), BF16),
                        pltpu.VMEM((1, dff, d), BF16)],
    )
    return pl.pallas_call(
        functools.partial(_moe_kernel, blk=blk),
        grid_spec=grid_spec,
        out_shape=jax.ShapeDtypeStruct((n_rows, d), F32),
        compiler_params=_cparams(("arbitrary",)),
    )(blk_eid, tok3, tok3, x, g_ffn.reshape(1, d), w_gate, w_up, w_down)


def _combine_kernel(pos_ref, pos_next_ref, x_ref, rw_ref, y_hbm, g_ref, o_ref, ybuf, sem,
                    *, tc, final_norm):
    i = pl.program_id(0)
    n = pl.num_programs(0)
    slot = i % 2

    @pl.when(i == 0)
    def _():
        _gather_rows(pos_ref, 2 * tc, y_hbm, ybuf.at[0], sem.at[0])

    _wait_rows(2 * tc, y_hbm, ybuf.at[slot], sem.at[slot])

    for r in range(2 * tc):
        _token_copy(y_hbm, ybuf.at[1 - slot], sem.at[1 - slot], pos_next_ref[0, 0, r], r,
                    1).start(priority=r % 2)

    rw = rw_ref[...]
    y = x_ref[...] + rw[:, 0:1] * ybuf[slot, pl.ds(0, tc), :] + rw[:, 1:2] * ybuf[slot, pl.ds(tc, tc), :]
    if final_norm:
        ms = jnp.mean(y * y, axis=-1, keepdims=True)
        y = (y * lax.rsqrt(ms + RMS_EPS)) * g_ref[...]
    o_ref[...] = y

    @pl.when(i == n - 1)
    def _():
        _wait_rows(2 * tc, y_hbm, ybuf.at[1 - slot], sem.at[1 - slot])


def _combine(x, route_w, pos, y_sorted, g_final, *, final_norm, tc=256):
    t, d = x.shape
    tc = min(tc, t)
    assert t % tc == 0
    nt = t // tc
    pos3 = pos.reshape(nt, tc, 2).transpose(0, 2, 1).reshape(nt, 1, 2 * tc)
    return pl.pallas_call(
        functools.partial(_combine_kernel, tc=tc, final_norm=final_norm),
        grid=(nt,),
        in_specs=[
            pl.BlockSpec((1, 1, 2 * tc), lambda i: (i, 0, 0), memory_space=pltpu.SMEM),
            pl.BlockSpec((1, 1, 2 * tc), lambda i: (jnp.minimum(i + 1, nt - 1), 0, 0),
                         memory_space=pltpu.SMEM),
            pl.BlockSpec((tc, d), lambda i: (i, 0)),
            pl.BlockSpec((tc, LANES), lambda i: (i, 0)),
            pl.BlockSpec(memory_space=pl.ANY),
            pl.BlockSpec((1, d), lambda i: (0, 0)),
        ],
        out_specs=pl.BlockSpec((tc, d), lambda i: (i, 0)),
        out_shape=jax.ShapeDtypeStruct((t, d), F32),
        scratch_shapes=[pltpu.VMEM((2, 2 * tc, d), F32), pltpu.SemaphoreType.DMA((2,))],
        compiler_params=_cparams(("arbitrary",)),
    )(pos3, pos3, x, route_w, y_sorted, g_final.reshape(1, d))


def _dispatch(route_i, blk):
    t = route_i.shape[0]
    eid = route_i[:, :2].reshape(-1)
    n_assign = eid.shape[0]
    onehot = (eid[:, None] == jnp.arange(MOE_N_EXPERTS, dtype=I32)[None, :]).astype(F32)
    tile = 256
    oh = onehot.reshape(n_assign // tile, tile, MOE_N_EXPERTS)
    within = jnp.einsum('ij,tje->tie', jnp.tril(jnp.ones((tile, tile), F32)), oh)
    tile_tot = within[:, -1, :]
    tile_off = jnp.cumsum(tile_tot, axis=0) - tile_tot
    csum = within + tile_off[:, None, :]
    rank = (jnp.sum(csum * oh, axis=2) - 1.0).reshape(n_assign).astype(I32)
    counts = (tile_off[-1] + tile_tot[-1]).astype(I32)
    padded = ((counts + blk - 1) // blk) * blk
    pend = jnp.cumsum(padded)
    pstart = pend - padded
    dest = (pstart[eid] + rank).astype(I32)
    n_rows = n_assign + MOE_N_EXPERTS * blk
    n_blocks = n_rows // blk
    row_tok = jnp.zeros((n_rows,), I32).at[dest].set(jnp.arange(n_assign, dtype=I32) // 2)
    blk_start = jnp.arange(n_blocks, dtype=I32) * blk
    blk_eid = jnp.minimum(jnp.sum((pend[None, :] <= blk_start[:, None]).astype(I32), axis=1),
                          MOE_N_EXPERTS - 1)
    return row_tok, blk_eid, dest.reshape(t, 2)


def _router_weights(w_group, w_expert):
    d = w_group.shape[0]
    w = jnp.zeros((d, LANES), F32)
    w = w.at[:, :MOE_GROUPS].set(w_group).at[:, MOE_GROUPS:MOE_GROUPS + MOE_N_EXPERTS].set(w_expert)
    return w.astype(BF16)


def _moe_layer(x1, route_i, route_w, g_ffn, w_gate, w_up, w_down, g_final, *, layer, final_norm,
               blk=256):
    row_tok, blk_eid, pos = _dispatch(route_i, blk)
    stack = lambda w: w.reshape((-1,) + w.shape[2:])
    y_sorted = _moe_ffn(x1, g_ffn, blk_eid, row_tok, stack(w_gate), stack(w_up),
                        stack(w_down), blk=blk, layer=layer)
    return _combine(x1, route_w, pos, y_sorted, g_final, final_norm=final_norm)


def kernel(x, norm_mix, norm_ffn, gla_w_in, gla_w_gate_up, gla_b_gate, gla_norm, gla_w_out,
           diff_w_in, diff_lambda_q1, diff_lambda_k1, diff_lambda_q2, diff_lambda_k2,
           diff_subln, diff_w_out, moe_w_group, moe_w_expert, moe_w_gate, moe_w_up,
           moe_w_down, final_norm):
    batch, seq, d = x.shape
    t = batch * seq
    xf = x.reshape(t, d)

    dk_total = d // 2
    n_main = 2 * dk_total + 2 * d
    w_in = gla_w_in[0]
    gla_dk = dk_total // GLA_HEADS
    colscale = jnp.ones((n_main,), F32).at[:dk_total].set(gla_dk ** -0.5)
    rank = w_in.shape[1] - n_main
    w_gz = jnp.zeros((d, LANES), F32).at[:, :rank].set(w_in[:, n_main:]).astype(BF16)
    proj, gz = _norm_matmul(xf, norm_mix[0], w_in[:, :n_main].astype(BF16), colscale, w_gz)
    o = _gla_scan(proj, gz, gla_w_gate_up[0], gla_b_gate[0], gla_norm[0], batch=batch, seq=seq)
    x1, ri, rw = _proj_router(o, gla_w_out[0].astype(BF16), xf, norm_ffn[0],
                              _router_weights(moe_w_group[0], moe_w_expert[0]))
    x2 = _moe_layer(x1, ri, rw, norm_ffn[0], moe_w_gate, moe_w_up, moe_w_down,
                    final_norm, layer=0, final_norm=False)

    hd = d // DIFF_HEADS // 2
    colscale = jnp.ones((3 * d,), F32).at[:d].set(hd ** -0.5 * LOG2E)
    qkv = _norm_matmul(x2, norm_mix[1], diff_w_in[0].astype(BF16), colscale)
    o = _diff_attention(qkv, diff_lambda_q1[0], diff_lambda_k1[0], diff_lambda_q2[0],
                        diff_lambda_k2[0], diff_subln[0], batch=batch, seq=seq, layer_idx=1)
    x3, ri, rw = _proj_router(o, diff_w_out[0].astype(BF16), x2, norm_ffn[1],
                              _router_weights(moe_w_group[1], moe_w_expert[1]))
    out = _moe_layer(x3, ri, rw, norm_ffn[1], moe_w_gate, moe_w_up, moe_w_down,
                     final_norm, layer=1, final_norm=True)
    return out.reshape(batch, seq, d)
```
